```python
import math
import jax, jax.numpy as jnp
from jax import lax
import numpy as np

D_MODEL = 1024
BATCH = 2
SEQ = 8192
DEPTH = 2
DEC_BATCH = 8
DEC_SEQ = 16
PAST_LEN = 1024

CHUNK = 64
HEAD_DIM = 64
A_HEADS = D_MODEL // 128
A_WIDTH = A_HEADS * HEAD_DIM
A_PREV_CHUNKS = 8
A_BAND = A_PREV_CHUNKS * CHUNK
A_REL_CLIP = 128
B_CHUNK = 128
B_GROUPS = 4
B_WIDTH = D_MODEL // 2
B_GROUP_CH = B_WIDTH // B_GROUPS
C_HEADS = D_MODEL // 128
C_KV_HEADS = C_HEADS // 4
C_GROUP = C_HEADS // C_KV_HEADS
C_WIDTH = C_HEADS * HEAD_DIM
C_KV_WIDTH = C_KV_HEADS * HEAD_DIM
C_WINDOW = 128
C_PREV_CHUNKS = C_WINDOW // CHUNK
C_BAND = C_PREV_CHUNKS * CHUNK
T5_BUCKETS = 32
T5_MAX_DIST = 128
N_BRANCH = 3
BRANCH_WIDTH = A_WIDTH
IN_COLS = 3 * A_WIDTH + 2 * B_WIDTH + C_WIDTH + 2 * C_KV_WIDTH + N_BRANCH * D_MODEL
N_EXPERTS = 32
TOP_K = 4
D_EXPERT = D_MODEL
SWIGLU_ALPHA = 1.702
SWIGLU_LIMIT = 7.0
MOE_BLOCK = 128
RMS_EPS = 1e-6
LN_EPS = 1e-5
NEG_INF = -1e30

kernel_name = 'hybrid_chunk_streaming_encoder_step'


def rms_norm(x, g):
    xf = x.astype(jnp.float32)
    y = xf * lax.rsqrt(jnp.mean(xf * xf, axis=-1, keepdims=True) + RMS_EPS)
    return y.astype(x.dtype) * g


def layer_norm(x, g, b):
    xf = x.astype(jnp.float32)
    mu = jnp.mean(xf, axis=-1, keepdims=True)
    var = jnp.mean(jnp.square(xf - mu), axis=-1, keepdims=True)
    return ((xf - mu) * lax.rsqrt(var + LN_EPS)).astype(x.dtype) * g + b


def t5_bucket(rel):
    n = T5_BUCKETS // 2
    max_exact = n // 2
    ret = jnp.where(rel > 0, n, 0)
    a = jnp.abs(rel)
    large = max_exact + (jnp.log(jnp.maximum(a, 1).astype(jnp.float32) / max_exact)
                         / math.log(T5_MAX_DIST / max_exact) * (n - max_exact)).astype(jnp.int32)
    large = jnp.minimum(large, n - 1)
    return ret + jnp.where(a < max_exact, a, large)


def a_bias(a_rel, q_idx, k_idx):
    d = jnp.clip(q_idx[:, None] - k_idx[None, :], -A_REL_CLIP, A_REL_CLIP) + A_REL_CLIP
    return a_rel[:, d].astype(jnp.float32)[:, None]


def c_bias(t5_table, q_idx, k_idx):
    b = t5_table[t5_bucket(k_idx[None, :] - q_idx[:, None])]
    qn, kn = b.shape[0], b.shape[1]
    return jnp.transpose(b, (2, 0, 1)).astype(jnp.float32).reshape(C_KV_HEADS, C_GROUP, qn, kn)


def attn_probs(s, sinks):
    if sinks is None:
        return jax.nn.softmax(s, axis=-1)
    sk = sinks.astype(jnp.float32)[..., None, None]
    m = jnp.maximum(jnp.max(s, axis=-1, keepdims=True), sk)
    e = jnp.exp(s - m)
    return e / (jnp.sum(e, axis=-1, keepdims=True) + jnp.exp(sk - m))


def band_gather(t, n_prev):
    nc = t.shape[1]
    tp = jnp.pad(t, ((0, 0), (n_prev, 0)) + ((0, 0),) * (t.ndim - 2))
    return jnp.concatenate([tp[:, j:j + nc] for j in range(n_prev + 1)], axis=2)


def band_attn_prompt(q, k, v, n_prev, bias, sinks):
    bn, s_len, kvh, g, d = q.shape
    nc = s_len // CHUNK
    kb_len = (n_prev + 1) * CHUNK
    qc = q.reshape(bn, nc, CHUNK, kvh, g, d)
    kb = band_gather(k.reshape(bn, nc, CHUNK, kvh, d), n_prev)
    vb = band_gather(v.reshape(bn, nc, CHUNK, kvh, d), n_prev)
    s = jnp.einsum('bnqhgd,bnkhd->bnhgqk', qc, kb).astype(jnp.float32) * (d ** -0.5) + bias
    key_chunk = jnp.arange(nc)[:, None] - n_prev + jnp.arange(kb_len)[None, :] // CHUNK
    s = jnp.where((key_chunk >= 0)[None, :, None, None, None, :], s, NEG_INF)
    p = attn_probs(s, sinks).astype(v.dtype)
    o = jnp.einsum('bnhgqk,bnkhd->bnqhgd', p, vb)
    return o.reshape(bn, s_len, kvh * g * d)


def attn_sample(q, k_all, v_all, bias, sinks):
    bn, l_len, kvh, g, d = q.shape
    s = jnp.einsum('bqhgd,bkhd->bhgqk', q, k_all).astype(jnp.float32) * (d ** -0.5) + bias
    p = attn_probs(s, sinks).astype(v_all.dtype)
    return jnp.einsum('bhgqk,bkhd->bqhgd', p, v_all).reshape(bn, l_len, kvh * g * d)


def chunk_causal_mask(n):
    i = jnp.arange(n)
    return ((i[None, :] // CHUNK) <= (i[:, None] // CHUNK)).astype(jnp.float32)


def sgu(u, v, ws, bs):
    bn, s_len, _ = u.shape
    n = min(s_len, B_CHUNK)
    w = (ws * chunk_causal_mask(B_CHUNK).astype(ws.dtype))[:, :n, :n]
    vg = v.reshape(bn, s_len // n, n, B_GROUPS, B_GROUP_CH)
    mixed = jnp.einsum('gij,bnjgc->bnigc', w, vg) + bs[:, :n].T[:, :, None]
    return u * mixed.reshape(bn, s_len, B_WIDTH)


def head_rms(t, g):
    return rms_norm(t, g)


def project(h, w_in, a_qn, a_kn, b_lng, b_lnb, c_qn, c_kn):
    bn, s_len, _ = h.shape
    sizes = [A_WIDTH, A_WIDTH, A_WIDTH, B_WIDTH, B_WIDTH, C_WIDTH, C_KV_WIDTH, C_KV_WIDTH]
    aq, ak, av, bu, bv, cq, ck, cv, g = jnp.split(h @ w_in, np.cumsum(sizes).tolist(), axis=-1)
    aq = head_rms(aq.reshape(bn, s_len, A_HEADS, 1, HEAD_DIM), a_qn)
    ak = head_rms(ak.reshape(bn, s_len, A_HEADS, HEAD_DIM), a_kn)
    av = av.reshape(bn, s_len, A_HEADS, HEAD_DIM)
    bu = jax.nn.gelu(bu)
    bv = layer_norm(jax.nn.gelu(bv), b_lng, b_lnb)
    cq = head_rms(cq.reshape(bn, s_len, C_KV_HEADS, C_GROUP, HEAD_DIM), c_qn)
    ck = head_rms(ck.reshape(bn, s_len, C_KV_HEADS, HEAD_DIM), c_kn)
    cv = cv.reshape(bn, s_len, C_KV_HEADS, HEAD_DIM)
    return aq, ak, av, bu, bv, cq, ck, cv, g


def merge(o_a, o_b, o_c, g, w_br, w_out):
    o = jnp.stack([o_a, o_b, o_c], axis=-2)
    gates = jax.nn.sigmoid(g.reshape(g.shape[:-1] + (N_BRANCH, D_MODEL)))
    br = jnp.einsum('bsrw,rwd->bsrd', o, w_br)
    return jnp.sum(gates * br, axis=-2) @ w_out


def moe_ffn(h, w_r, b_r, w_gu, b_gu, w_dn, b_dn):
    t_len = h.shape[0]
    logits = (h @ w_r + b_r).astype(jnp.float32)
    top_v, top_e = lax.top_k(logits, TOP_K)
    gate = jax.nn.softmax(top_v, axis=-1)
    n_assign = t_len * TOP_K
    flat_e = top_e.reshape(-1)
    flat_tok = jnp.arange(n_assign, dtype=jnp.int32) // TOP_K
    flat_gate = gate.reshape(-1)
    order = jnp.argsort(flat_e)
    sorted_e = flat_e[order]
    counts = jnp.bincount(flat_e, length=N_EXPERTS)
    padded = (counts + MOE_BLOCK - 1) // MOE_BLOCK * MOE_BLOCK
    pad_end = jnp.cumsum(padded)
    pad_start = pad_end - padded
    grp_start = jnp.cumsum(counts) - counts
    dest = pad_start[sorted_e] + jnp.arange(n_assign, dtype=jnp.int32) - grp_start[sorted_e]
    n_blocks = -(-(n_assign + N_EXPERTS * (MOE_BLOCK - 1)) // MOE_BLOCK)
    n_rows = n_blocks * MOE_BLOCK
    row_tok = jnp.full((n_rows,), t_len, jnp.int32).at[dest].set(flat_tok[order])
    row_gate = jnp.zeros((n_rows,), jnp.float32).at[dest].set(flat_gate[order])
    block_e = jnp.minimum(jnp.searchsorted(pad_end, jnp.arange(n_blocks) * MOE_BLOCK, side='right'),
                          N_EXPERTS - 1)
    h_pad = jnp.concatenate([h, jnp.zeros((1, h.shape[1]), h.dtype)], axis=0)
    xb = h_pad[row_tok].reshape(n_blocks, MOE_BLOCK, h.shape[1])

    def expert_block(args):
        xe, e = args
        gu = xe @ w_gu[e] + b_gu[e]
        g, u = gu[:, :D_EXPERT], gu[:, D_EXPERT:]
        g = jnp.minimum(g, SWIGLU_LIMIT)
        u = jnp.clip(u, -SWIGLU_LIMIT, SWIGLU_LIMIT)
        act = (u + 1) * g * jax.nn.sigmoid(SWIGLU_ALPHA * g)
        return act @ w_dn[e] + b_dn[e]

    yb = lax.map(expert_block, (xb, block_e)).reshape(n_rows, h.shape[1])
    y = jnp.zeros((t_len + 1, h.shape[1]), h.dtype).at[row_tok].add(yb * row_gate[:, None].astype(h.dtype))
    return y[:t_len]


def trunk_layer(x, c, lp, t5_bias, caches):
    (w_ada, b_ada, g_mix, g_ffn, w_in, a_qn, a_kn, a_rel, b_lng, b_lnb, b_ws, b_bs,
     c_qn, c_kn, c_sink, w_br, w_out, w_r, b_r, w_gu, b_gu, w_dn, b_dn) = lp
    bn, s_len, _ = x.shape
    mod = (jax.nn.silu(c) @ w_ada + b_ada).reshape(bn, 6, 1, D_MODEL)
    shift1, scale1, gate1, shift2, scale2, gate2 = [mod[:, i] for i in range(6)]
    h = rms_norm(x, g_mix) * (1 + scale1) + shift1
    aq, ak, av, bu, bv, cq, ck, cv, g = project(h, w_in, a_qn, a_kn, b_lng, b_lnb, c_qn, c_kn)
    sinks = c_sink.reshape(C_KV_HEADS, C_GROUP)
    if caches is None:
        qa_idx = A_PREV_CHUNKS * CHUNK + jnp.arange(CHUNK)
        ka_idx = jnp.arange((A_PREV_CHUNKS + 1) * CHUNK)
        o_a = band_attn_prompt(aq, ak, av, A_PREV_CHUNKS, a_bias(a_rel, qa_idx, ka_idx), None)
        qc_idx = C_PREV_CHUNKS * CHUNK + jnp.arange(CHUNK)
        kc_idx = jnp.arange((C_PREV_CHUNKS + 1) * CHUNK)
        o_c = band_attn_prompt(cq, ck, cv, C_PREV_CHUNKS, c_bias(t5_bias, qc_idx, kc_idx), sinks)
        na, nc = min(A_BAND, s_len), min(C_BAND, s_len)
        new = (ak[:, s_len - na:], av[:, s_len - na:], ck[:, s_len - nc:], cv[:, s_len - nc:])
    else:
        ca_k, ca_v, cc_k, cc_v = caches
        ra, rc = ca_k.shape[1], cc_k.shape[1]
        o_a = attn_sample(aq, jnp.concatenate([ca_k, ak], axis=1), jnp.concatenate([ca_v, av], axis=1),
                          a_bias(a_rel, ra + jnp.arange(s_len), jnp.arange(ra + s_len)), None)
        o_c = attn_sample(cq, jnp.concatenate([cc_k, ck], axis=1), jnp.concatenate([cc_v, cv], axis=1),
                          c_bias(t5_bias, rc + jnp.arange(s_len), jnp.arange(rc + s_len)), sinks)
        new = (ak, av, ck, cv, bv)
    o_b = sgu(bu, bv, b_ws, b_bs)
    x = x + gate1 * merge(o_a, o_b, o_c, g, w_br, w_out)
    h2 = rms_norm(x, g_ffn) * (1 + scale2) + shift2
    ff = moe_ffn(h2.reshape(bn * s_len, D_MODEL), w_r, b_r, w_gu, b_gu, w_dn, b_dn)
    x = x + gate2 * ff.reshape(bn, s_len, D_MODEL)
    return x, new


def setup_inputs(seed: int = 0) -> dict:
    key = jax.random.key(seed)
    ks = iter(jax.random.split(key, 40))

    def nrm(shape, scale):
        return jax.random.normal(next(ks), shape, jnp.float32) * scale

    a_rows = min(A_BAND, PAST_LEN)
    c_rows = min(C_BAND, PAST_LEN)
    return {
        'x_prompt': nrm((BATCH, SEQ, D_MODEL), 1.0),
        'x_sample': nrm((DEC_BATCH, DEC_SEQ, D_MODEL), 1.0),
        'cache_a_k': nrm((DEPTH, DEC_BATCH, a_rows, A_HEADS, HEAD_DIM), 1.0),
        'cache_a_v': nrm((DEPTH, DEC_BATCH, a_rows, A_HEADS, HEAD_DIM), 1.0),
        'cache_c_k': nrm((DEPTH, DEC_BATCH, c_rows, C_KV_HEADS, HEAD_DIM), 1.0),
        'cache_c_v': nrm((DEPTH, DEC_BATCH, c_rows, C_KV_HEADS, HEAD_DIM), 1.0),
        'c_prompt': nrm((BATCH, D_MODEL), 1.0),
        'c_sample': nrm((DEC_BATCH, D_MODEL), 1.0),
        'w_ada': nrm((DEPTH, D_MODEL, 6 * D_MODEL), 0.5 * D_MODEL ** -0.5),
        'b_ada': nrm((DEPTH, 6 * D_MODEL), 0.01),
        'g_norm_mix': 1.0 + nrm((DEPTH, D_MODEL), 0.02),
        'g_norm_ffn': 1.0 + nrm((DEPTH, D_MODEL), 0.02),
        'w_in': nrm((DEPTH, D_MODEL, IN_COLS), D_MODEL ** -0.5),
        'a_q_norm': 1.0 + nrm((DEPTH, HEAD_DIM), 0.02),
        'a_k_norm': 1.0 + nrm((DEPTH, HEAD_DIM), 0.02),
        'a_rel_bias': nrm((DEPTH, A_HEADS, 2 * A_REL_CLIP + 1), 0.5),
        'b_ln_g': 1.0 + nrm((DEPTH, B_WIDTH), 0.02),
        'b_ln_b': nrm((DEPTH, B_WIDTH), 0.02),
        'b_w_s': nrm((DEPTH, B_GROUPS, B_CHUNK, B_CHUNK), B_CHUNK ** -0.5),
        'b_b_s': 1.0 + nrm((DEPTH, B_GROUPS, B_CHUNK), 0.02),
        'c_q_norm': 1.0 + nrm((DEPTH, HEAD_DIM), 0.02),
        'c_k_norm': 1.0 + nrm((DEPTH, HEAD_DIM), 0.02),
        'c_sinks': nrm((DEPTH, C_HEADS), 0.5),
        't5_bias': nrm((T5_BUCKETS, C_HEADS), 0.5),
        'w_branch': nrm((DEPTH, N_BRANCH, BRANCH_WIDTH, D_MODEL), BRANCH_WIDTH ** -0.5),
        'w_out': nrm((DEPTH, D_MODEL, D_MODEL), D_MODEL ** -0.5),
        'w_router': nrm((DEPTH, D_MODEL, N_EXPERTS), D_MODEL ** -0.5),
        'b_router': nrm((DEPTH, N_EXPERTS), 0.01),
        'w_gate_up': nrm((DEPTH, N_EXPERTS, D_MODEL, 2 * D_EXPERT), D_MODEL ** -0.5),
        'b_gate_up': nrm((DEPTH, N_EXPERTS, 2 * D_EXPERT), 0.01),
        'w_down': nrm((DEPTH, N_EXPERTS, D_EXPERT, D_MODEL), D_EXPERT ** -0.5),
        'b_down': nrm((DEPTH, N_EXPERTS, D_MODEL), 0.01),
    }


def reference(x_prompt, x_sample, cache_a_k, cache_a_v, cache_c_k, cache_c_v, c_prompt, c_sample,
              w_ada, b_ada, g_norm_mix, g_norm_ffn, w_in, a_q_norm, a_k_norm, a_rel_bias,
              b_ln_g, b_ln_b, b_w_s, b_b_s, c_q_norm, c_k_norm, c_sinks, t5_bias,
              w_branch, w_out, w_router, b_router, w_gate_up, b_gate_up, w_down, b_down):
    xp, xs = x_prompt, x_sample
    st_p, st_s = [], []
    for l in range(DEPTH):
        lp = (w_ada[l], b_ada[l], g_norm_mix[l], g_norm_ffn[l], w_in[l], a_q_norm[l], a_k_norm[l],
              a_rel_bias[l], b_ln_g[l], b_ln_b[l], b_w_s[l], b_b_s[l], c_q_norm[l], c_k_norm[l],
              c_sinks[l], w_branch[l], w_out[l], w_router[l], b_router[l], w_gate_up[l],
              b_gate_up[l], w_down[l], b_down[l])
        xp, new_p = trunk_layer(xp, c_prompt, lp, t5_bias, None)
        xs, new_s = trunk_layer(xs, c_sample, lp, t5_bias,
                                (cache_a_k[l], cache_a_v[l], cache_c_k[l], cache_c_v[l]))
        st_p.append(new_p)
        st_s.append(new_s)

    def stack(rows, i):
        return jnp.stack([r[i] for r in rows], axis=0)

    return (xp, xs,
            stack(st_p, 0), stack(st_p, 1), stack(st_p, 2), stack(st_p, 3),
            stack(st_s, 0), stack(st_s, 1), stack(st_s, 2), stack(st_s, 3), stack(st_s, 4))
```

```python
import functools

import jax
import jax.numpy as jnp
import numpy as np
from jax import lax
from jax.experimental import pallas as pl
from jax.experimental.pallas import tpu as pltpu

F32 = jnp.float32
BF16 = jnp.bfloat16

CHUNK = 64
HEAD_DIM = 64
A_PREV_CHUNKS = 8
A_REL_CLIP = 128
B_CHUNK = 128
B_GROUPS = 4
C_PREV_CHUNKS = 2
T5_BUCKETS = 32
T5_MAX_DIST = 128
N_BRANCH = 3
N_EXPERTS = 32
TOP_K = 4
SWIGLU_ALPHA = 1.702
SWIGLU_LIMIT = 7.0
RMS_EPS = 1e-6
LN_EPS = 1e-5
NEG_INF = -1e30
F32_LOWEST = float(np.finfo(np.float32).min)

VMEM_LIMIT_BYTES = 56 * 1024 * 1024
ADA_ROWS = 16
PROJ_TILE = 256
A_Q_TILE = 256
C_Q_TILE = 128
MOE_ROWS = 256
MOE_TOKEN_TILE = 256


def _cparams(n_axes, vmem=None):
    return pltpu.CompilerParams(dimension_semantics=("arbitrary",) * n_axes,
                                vmem_limit_bytes=vmem)


def _const_spec(shape):
    nd = len(shape)
    return pl.BlockSpec(shape, lambda *_: (0,) * nd)


def _ada_kernel(c_ref, w_ref, b_ref, o_ref):
    c = c_ref[...]
    s = c * jax.nn.sigmoid(c)
    o_ref[0] = jnp.dot(s, w_ref[0], preferred_element_type=F32,
                       precision=lax.Precision.HIGHEST) + b_ref[0]


def _ada_modulation(c_rows, w_ada, b_ada):
    depth, d, _ = w_ada.shape
    return pl.pallas_call(
        _ada_kernel,
        grid=(depth, 6),
        in_specs=[pl.BlockSpec((ADA_ROWS, d), lambda l, j: (0, 0)),
                  pl.BlockSpec((1, d, d), lambda l, j: (l, 0, j)),
                  pl.BlockSpec((1, 1, d), lambda l, j: (l, 0, j))],
        out_specs=pl.BlockSpec((1, ADA_ROWS, d), lambda l, j: (l, 0, j)),
        out_shape=jax.ShapeDtypeStruct((depth, ADA_ROWS, 6 * d), F32),
        compiler_params=_cparams(2),
    )(c_rows, w_ada, b_ada.reshape(depth, 1, 6 * d))


def _row_rms(x, gain):
    return x * lax.rsqrt(jnp.mean(x * x, axis=-1, keepdims=True) + RMS_EPS) * gain


def _head_rms(t, gain, bd):
    sq = t * t
    hi = sq.astype(BF16)
    lo = (sq - hi.astype(F32)).astype(BF16)
    ss = (jnp.dot(hi, bd, preferred_element_type=F32)
          + jnp.dot(lo, bd, preferred_element_type=F32))
    return t * lax.rsqrt(ss * (1.0 / HEAD_DIM) + RMS_EPS) * gain


def _proj_kernel(x_ref, shift_ref, scale_ref, gmix_ref,
                 w_aq, w_ak, w_av, w_bu, w_bv, w_cq, w_ck, w_cv, w_g,
                 bd_ref, aqn, akn, cqn, ckn, lng, lnb,
                 aq_o, ak_o, av_o, bu_o, bv_o, cq_o, ck_o, cv_o, g_o):
    x = x_ref[0]
    h = _row_rms(x, gmix_ref[...]) * (1.0 + scale_ref[0]) + shift_ref[0]
    hb = h.astype(BF16)

    def mm(w_ref):
        return jnp.dot(hb, w_ref[...], preferred_element_type=F32)

    bd = bd_ref[...]
    q_scale = HEAD_DIM ** -0.5
    aq_o[0] = (_head_rms(mm(w_aq), aqn[...], bd) * q_scale).astype(BF16)
    ak_o[0] = _head_rms(mm(w_ak), akn[...], bd)
    av_o[0] = mm(w_av)
    bu_o[0] = jax.nn.gelu(mm(w_bu)).astype(BF16)
    bv = jax.nn.gelu(mm(w_bv))
    mu = jnp.mean(bv, axis=-1, keepdims=True)
    var = jnp.mean(jnp.square(bv - mu), axis=-1, keepdims=True)
    bv_o[0] = (bv - mu) * lax.rsqrt(var + LN_EPS) * lng[...] + lnb[...]
    cq_o[0] = (_head_rms(mm(w_cq), cqn[...], bd) * q_scale).astype(BF16)
    kvw = ck_o.shape[-1]
    ck_o[0] = _head_rms(mm(w_ck), ckn[...], bd[:kvw, :kvw])
    cv_o[0] = mm(w_cv)
    d = x.shape[-1]
    for r in range(N_BRANCH):
        g_o[0, :, r * d:(r + 1) * d] = jax.nn.sigmoid(
            jnp.dot(hb, w_g[:, r * d:(r + 1) * d], preferred_element_type=F32)).astype(BF16)


def _project(x, shift, scale, lw, tm):
    b, s, d = x.shape
    r = shift.shape[1]
    widths = [w.shape[1] for w in lw["w_split"]]
    aw, kvw = widths[0], widths[6]
    tok = lambda w: pl.BlockSpec((1, tm, w), lambda bi, i: (bi, i, 0))
    mod = pl.BlockSpec((1, r, d), lambda bi, i: (bi, 0, 0))
    in_specs = ([tok(d), mod, mod, _const_spec((1, d))]
                + [_const_spec(w.shape) for w in lw["w_split"]]
                + [_const_spec(lw["bd"].shape)]
                + [_const_spec((1, aw)), _const_spec((1, aw)), _const_spec((1, aw)),
                   _const_spec((1, kvw)), _const_spec((1, aw)), _const_spec((1, aw))])
    out_dt = [BF16, F32, F32, BF16, F32, BF16, F32, F32, BF16]
    return pl.pallas_call(
        _proj_kernel,
        grid=(b, s // tm),
        in_specs=in_specs,
        out_specs=[tok(w) for w in widths],
        out_shape=[jax.ShapeDtypeStruct((b, s, w), dt) for w, dt in zip(widths, out_dt)],
        compiler_params=_cparams(2, VMEM_LIMIT_BYTES),
    )(x, shift, scale, lw["g_mix"], *lw["w_split"], lw["bd"],
      lw["aqn"], lw["akn"], lw["cqn"], lw["ckn"], lw["lng"], lw["lnb"])


def _attn_kernel(*refs, n_q_heads, n_kv_heads, has_sink, carry):
    refs = list(refs)
    q_ref, k_ref, v_ref = refs[:3]
    pos = 3
    if not carry:
        pk_ref, pv_ref = refs[pos:pos + 2]
        pos += 2
    bias_ref = refs[pos]
    pos += 1
    if has_sink:
        sink_ref = refs[pos]
        pos += 1
    o_ref = refs[pos]
    pos += 1
    if carry:
        pk_scr, pv_scr = refs[pos:pos + 2]

    j = pl.program_id(1)
    kc = k_ref[0].astype(BF16)
    vc = v_ref[0].astype(BF16)
    tq = kc.shape[0]
    if carry:
        @pl.when(j == 0)
        def _():
            pk_scr[...] = jnp.zeros_like(pk_scr)
            pv_scr[...] = jnp.zeros_like(pv_scr)
        kp, vp = pk_scr[...], pv_scr[...]
    else:
        kp, vp = pk_ref[0].astype(BF16), pv_ref[0].astype(BF16)
    n_prev = kp.shape[0]
    kw = jnp.concatenate([kp, kc], axis=0)
    vw = jnp.concatenate([vp, vc], axis=0)
    width = n_prev + tq
    if carry:
        before_start = lax.broadcasted_iota(jnp.int32, (tq, width), 1) < (n_prev - j * tq)
    q = q_ref[0]
    group = n_q_heads // n_kv_heads
    outs = []
    for h in range(n_q_heads):
        kh = h // group
        qh = q[:, h * HEAD_DIM:(h + 1) * HEAD_DIM]
        s = lax.dot_general(qh, kw[:, kh * HEAD_DIM:(kh + 1) * HEAD_DIM],
                            (((1,), (1,)), ((), ())), preferred_element_type=F32) + bias_ref[h]
        if carry:
            s = jnp.where(before_start, NEG_INF, s)
        m = jnp.max(s, axis=-1, keepdims=True)
        if has_sink:
            sink = sink_ref[h]
            m = jnp.maximum(m, sink)
        e = jnp.exp(s - m)
        denom = jnp.sum(e, axis=-1, keepdims=True)
        if has_sink:
            denom = denom + jnp.exp(sink - m)
        o = jnp.dot(e.astype(BF16), vw[:, kh * HEAD_DIM:(kh + 1) * HEAD_DIM],
                    preferred_element_type=F32)
        outs.append(o / denom)
    o_ref[0] = jnp.concatenate(outs, axis=-1).astype(BF16)
    if carry:
        pk_scr[...] = kw[tq:, :]
        pv_scr[...] = vw[tq:, :]


def _band_attention(q, k, v, bias, sinks, n_prev_rows, tq, prev=None):
    b, s, qw = q.shape
    kvw = k.shape[-1]
    hq, kvh = qw // HEAD_DIM, kvw // HEAD_DIM
    carry = prev is None
    width = n_prev_rows + tq
    tok = lambda w: pl.BlockSpec((1, tq, w), lambda bi, i: (bi, i, 0))
    in_specs = [tok(qw), tok(kvw), tok(kvw)]
    args = [q, k, v]
    if not carry:
        pspec = pl.BlockSpec((1, n_prev_rows, kvw), lambda bi, i: (bi, 0, 0))
        in_specs += [pspec, pspec]
        args += list(prev)
    in_specs.append(_const_spec((hq, tq, width)))
    args.append(bias)
    if sinks is not None:
        in_specs.append(pl.BlockSpec(memory_space=pltpu.SMEM))
        args.append(sinks)
    scratch = [pltpu.VMEM((n_prev_rows, kvw), BF16)] * 2 if carry else []
    return pl.pallas_call(
        functools.partial(_attn_kernel, n_q_heads=hq, n_kv_heads=kvh,
                          has_sink=sinks is not None, carry=carry),
        grid=(b, s // tq),
        in_specs=in_specs,
        out_specs=tok(qw),
        out_shape=jax.ShapeDtypeStruct((b, s, qw), BF16),
        scratch_shapes=scratch,
        compiler_params=_cparams(2, VMEM_LIMIT_BYTES),
    )(*args)


def _band_bias(base, tq, n_prev_chunks):
    width = n_prev_chunks * CHUNK + tq
    r = np.arange(tq)[:, None]
    w = np.arange(width)[None, :]
    rel_chunk = w // CHUNK - r // CHUNK
    valid = (rel_chunk >= 0) & (rel_chunk <= n_prev_chunks)
    col = np.where(valid, rel_chunk * CHUNK + w % CHUNK, 0)
    row = np.broadcast_to(r % CHUNK, col.shape)
    return jnp.where(valid[None], base[:, row, col], NEG_INF)


def _a_bias_table(a_rel, q_idx, k_idx):
    d = jnp.clip(q_idx[:, None] - k_idx[None, :], -A_REL_CLIP, A_REL_CLIP) + A_REL_CLIP
    return a_rel[:, d].astype(F32)


def _t5_bucket(rel):
    n = T5_BUCKETS // 2
    max_exact = n // 2
    ret = jnp.where(rel > 0, n, 0)
    a = jnp.abs(rel)
    large = max_exact + (jnp.log(jnp.maximum(a, 1).astype(F32) / max_exact)
                         / np.log(T5_MAX_DIST / max_exact) * (n - max_exact)).astype(jnp.int32)
    large = jnp.minimum(large, n - 1)
    return ret + jnp.where(a < max_exact, a, large)


def _c_bias_table(t5_table, q_idx, k_idx):
    b = t5_table[_t5_bucket(k_idx[None, :] - q_idx[:, None])]
    return jnp.transpose(b, (2, 0, 1)).astype(F32)


def _merge_kernel(oa_ref, oc_ref, bu_ref, bv_ref, g_ref, x_ref, gate1_ref, shift2_ref, scale2_ref,
                  gffn_ref, wbr_ref, wout_ref, ws_ref, bsb_ref, wrt_ref, br_ref, tri_ref, cnt_ref,
                  xnew_o, h2_o, ids_o, gates_o, rank_o, cnt_o, base_scr):
    first = jnp.logical_and(pl.program_id(0) == 0, pl.program_id(1) == 0)

    @pl.when(first)
    def _():
        base_scr[...] = cnt_ref[...]

    tm, d = x_ref.shape[1], x_ref.shape[2]
    bw = bv_ref.shape[2]
    gch = bw // B_GROUPS
    bvb = bv_ref[0].astype(BF16)
    rows = []
    for r in range(tm // B_CHUNK):
        cols = [jnp.dot(ws_ref[gi], bvb[r * B_CHUNK:(r + 1) * B_CHUNK, gi * gch:(gi + 1) * gch],
                        preferred_element_type=F32) for gi in range(B_GROUPS)]
        rows.append(jnp.concatenate(cols, axis=-1) + bsb_ref[...])
    mixed = jnp.concatenate(rows, axis=0) if len(rows) > 1 else rows[0]
    ob = (bu_ref[0].astype(F32) * mixed).astype(BF16)

    branches = (oa_ref[0], ob, oc_ref[0])
    y = None
    for r in range(N_BRANCH):
        br = jnp.dot(branches[r], wbr_ref[r], preferred_element_type=F32)
        term = g_ref[0, :, r * d:(r + 1) * d].astype(F32) * br
        y = term if y is None else y + term
    out = jnp.dot(y.astype(BF16), wout_ref[...], preferred_element_type=F32)
    xn = x_ref[0] + gate1_ref[0] * out
    xnew_o[0] = xn
    h2 = _row_rms(xn, gffn_ref[...]) * (1.0 + scale2_ref[0]) + shift2_ref[0]
    h2_o[0] = h2

    logits = lax.dot_general(wrt_ref[...], h2, (((1,), (1,)), ((), ())),
                             preferred_element_type=F32,
                             precision=lax.Precision.HIGHEST) + br_ref[...]
    n_exp = logits.shape[0]
    eidx = lax.broadcasted_iota(jnp.int32, (n_exp, tm), 0).astype(F32)
    vals, ids = [], []
    rest = logits
    for _ in range(TOP_K):
        m = jnp.max(rest, axis=0, keepdims=True)
        sel = jnp.min(jnp.where(rest == m, eidx, float(n_exp)), axis=0, keepdims=True)
        vals.append(m)
        ids.append(sel)
        rest = jnp.where(eidx == sel, F32_LOWEST, rest)
    exps = [jnp.exp(v - vals[0]) for v in vals]
    tot = exps[0] + exps[1] + exps[2] + exps[3]
    gates_o[0] = jnp.concatenate([e / tot for e in exps], axis=0)
    ids_o[0] = jnp.concatenate(ids, axis=0).astype(jnp.int32)

    base = base_scr[...]
    ranks = []
    for k in range(TOP_K):
        hit = eidx == ids[k]
        onehot = jnp.where(hit, 1.0, 0.0)
        before = jnp.dot(onehot.astype(BF16), tri_ref[...], preferred_element_type=F32)
        ranks.append(jnp.sum(jnp.where(hit, base[:, 0:1] + before, 0.0), axis=0, keepdims=True))
        base = base + jnp.sum(onehot, axis=1, keepdims=True)
    rank_o[0] = jnp.concatenate(ranks, axis=0).astype(jnp.int32)
    base_scr[...] = base
    cnt_o[...] = base


def _merge(oa, oc, bu, bv, g, x, gate1, shift2, scale2, lw, sgu_w, sgu_b, counts, tm):
    b, s, d = x.shape
    r = gate1.shape[1]
    bw = bu.shape[-1]
    n_tiles = b * (s // tm)
    tok = lambda w: pl.BlockSpec((1, tm, w), lambda bi, i: (bi, i, 0))
    mod = pl.BlockSpec((1, r, d), lambda bi, i: (bi, 0, 0))
    per_tile = pl.BlockSpec((1, TOP_K, tm), lambda bi, i: (bi * (s // tm) + i, 0, 0))
    tri = jnp.asarray(np.triu(np.ones((tm, tm), np.float32), k=1), BF16)
    in_specs = [tok(bw), tok(bw), tok(bw), tok(bw), tok(N_BRANCH * d), tok(d), mod, mod, mod,
                _const_spec((1, d)), _const_spec(lw["w_br"].shape), _const_spec((d, d)),
                _const_spec(sgu_w.shape), _const_spec(sgu_b.shape),
                _const_spec((N_EXPERTS, d)), _const_spec((N_EXPERTS, 1)),
                _const_spec((tm, tm)), _const_spec((N_EXPERTS, 128))]
    out_shape = [jax.ShapeDtypeStruct((b, s, d), F32), jax.ShapeDtypeStruct((b, s, d), F32),
                 jax.ShapeDtypeStruct((n_tiles, TOP_K, tm), jnp.int32),
                 jax.ShapeDtypeStruct((n_tiles, TOP_K, tm), F32),
                 jax.ShapeDtypeStruct((n_tiles, TOP_K, tm), jnp.int32),
                 jax.ShapeDtypeStruct((N_EXPERTS, 128), F32)]
    out_specs = [tok(d), tok(d), per_tile, per_tile, per_tile, _const_spec((N_EXPERTS, 128))]
    return pl.pallas_call(
        _merge_kernel,
        grid=(b, s // tm),
        in_specs=in_specs,
        out_specs=out_specs,
        out_shape=out_shape,
        scratch_shapes=[pltpu.VMEM((N_EXPERTS, 128), F32)],
        compiler_params=_cparams(2, VMEM_LIMIT_BYTES),
    )(oa, oc, bu, bv, g, x, gate1, shift2, scale2, lw["g_ffn"], lw["w_br"], lw["w_out"],
      sgu_w, sgu_b, lw["w_rt"], lw["b_r"], tri, counts)


def _row_copy(src_ref, src_row, dst_ref, dst_row, sem):
    return pltpu.make_async_copy(src_ref.at[pl.ds(src_row, 1)], dst_ref.at[pl.ds(dst_row, 1)], sem)


def _dispatch_kernel(pos_ref, h_ref, xs_in_ref, xs_ref, sem):
    del xs_in_ref
    tm = h_ref.shape[0]
    base = pl.program_id(0) * (tm * TOP_K)

    def issue(t, carry):
        for k in range(TOP_K):
            _row_copy(h_ref, t, xs_ref, pos_ref[base + t * TOP_K + k], sem).start()
        return carry

    lax.fori_loop(0, tm, issue, 0)

    def drain(t, carry):
        for k in range(TOP_K):
            _row_copy(h_ref, 0, xs_ref, 0, sem).wait()
        return carry

    lax.fori_loop(0, tm, drain, 0)


def _dispatch(pos_flat, h, xs, tm):
    t, d = h.shape
    return pl.pallas_call(
        _dispatch_kernel,
        grid_spec=pltpu.PrefetchScalarGridSpec(
            num_scalar_prefetch=1,
            grid=(t // tm,),
            in_specs=[pl.BlockSpec((tm, d), lambda i, pos: (i, 0)),
                      pl.BlockSpec(memory_space=pl.ANY)],
            out_specs=pl.BlockSpec(memory_space=pl.ANY),
            scratch_shapes=[pltpu.SemaphoreType.DMA]),
        out_shape=jax.ShapeDtypeStruct(xs.shape, xs.dtype),
        input_output_aliases={2: 0},
        compiler_params=_cparams(1),
    )(pos_flat, h, xs)


def _expert_kernel(be_ref, nused_ref, xs_ref, wgu_ref, bgu_ref, wdn_ref, bdn_ref, ys_ref,
                   wgu_scr, wdn_scr):
    i = pl.program_id(0)
    prev_e = be_ref[jnp.maximum(i - 1, 0)]
    new_expert = jnp.logical_or(i == 0, be_ref[i] != prev_e)
    active = i < nused_ref[0]

    @pl.when(jnp.logical_and(active, new_expert))
    def _():
        wgu_scr[...] = wgu_ref[0].astype(BF16)
        wdn_scr[...] = wdn_ref[0].astype(BF16)

    @pl.when(active)
    def _():
        de = wdn_scr.shape[0]
        gu = jnp.dot(xs_ref[...].astype(BF16), wgu_scr[...],
                     preferred_element_type=F32) + bgu_ref[0]
        g = jnp.minimum(gu[:, :de], SWIGLU_LIMIT)
        u = jnp.clip(gu[:, de:], -SWIGLU_LIMIT, SWIGLU_LIMIT)
        act = (u + 1.0) * g * jax.nn.sigmoid(SWIGLU_ALPHA * g)
        ys_ref[...] = jnp.dot(act.astype(BF16), wdn_scr[...],
                              preferred_element_type=F32) + bdn_ref[0]

    @pl.when(jnp.logical_not(active))
    def _():
        ys_ref[...] = jnp.zeros_like(ys_ref)


def _experts(block_e, n_used, xs, w_gu, b_gu, w_dn, b_dn, rows):
    n_rows, d = xs.shape
    n_e, _, two_de = w_gu.shape
    de = two_de // 2
    n_blocks = n_rows // rows
    row_map = lambda i, be, nu: (jnp.minimum(i, nu[0] - 1), 0)
    e_map = lambda i, be, nu: (be[jnp.minimum(i, nu[0] - 1)], 0, 0)
    return pl.pallas_call(
        _expert_kernel,
        grid_spec=pltpu.PrefetchScalarGridSpec(
            num_scalar_prefetch=2,
            grid=(n_blocks,),
            in_specs=[pl.BlockSpec((rows, d), row_map),
                      pl.BlockSpec((1, d, two_de), e_map),
                      pl.BlockSpec((1, 1, two_de), e_map),
                      pl.BlockSpec((1, de, d), e_map),
                      pl.BlockSpec((1, 1, d), e_map)],
            out_specs=pl.BlockSpec((rows, d), lambda i, be, nu: (i, 0)),
            scratch_shapes=[pltpu.VMEM((d, two_de), BF16), pltpu.VMEM((de, d), BF16)]),
        out_shape=jax.ShapeDtypeStruct((n_rows, d), F32),
        compiler_params=_cparams(1, VMEM_LIMIT_BYTES),
    )(block_e, n_used, xs, w_gu, b_gu.reshape(n_e, 1, two_de), w_dn, b_dn.reshape(n_e, 1, d))


def _combine_kernel(pos_ref, ys_ref, gates_ref, x_ref, gate2_ref, o_ref, buf, sem):
    tm = x_ref.shape[1]
    n_i = pl.num_programs(1)
    base = (pl.program_id(0) * n_i + pl.program_id(1)) * (tm * TOP_K)

    def issue(t, carry):
        for k in range(TOP_K):
            _row_copy(ys_ref, pos_ref[base + t * TOP_K + k], buf.at[k], t, sem).start()
        return carry

    lax.fori_loop(0, tm, issue, 0)

    def drain(t, carry):
        for k in range(TOP_K):
            _row_copy(ys_ref, 0, buf.at[k], 0, sem).wait()
        return carry

    lax.fori_loop(0, tm, drain, 0)

    gates = gates_ref[0]
    ff = gates[:, 0:1] * buf[0]
    for k in range(1, TOP_K):
        ff = ff + gates[:, k:k + 1] * buf[k]
    o_ref[0] = x_ref[0] + gate2_ref[0] * ff


def _combine(pos_flat, ys, gates_tk, x, gate2, tm):
    b, s, d = x.shape
    r = gate2.shape[1]
    return pl.pallas_call(
        _combine_kernel,
        grid_spec=pltpu.PrefetchScalarGridSpec(
            num_scalar_prefetch=1,
            grid=(b, s // tm),
            in_specs=[pl.BlockSpec(memory_space=pl.ANY),
                      pl.BlockSpec((1, tm, TOP_K), lambda bi, i, pos: (bi, i, 0)),
                      pl.BlockSpec((1, tm, d), lambda bi, i, pos: (bi, i, 0)),
                      pl.BlockSpec((1, r, d), lambda bi, i, pos: (bi, 0, 0))],
            out_specs=pl.BlockSpec((1, tm, d), lambda bi, i, pos: (bi, i, 0)),
            scratch_shapes=[pltpu.VMEM((TOP_K, tm, d), F32), pltpu.SemaphoreType.DMA]),
        out_shape=jax.ShapeDtypeStruct((b, s, d), F32),
        compiler_params=_cparams(2, VMEM_LIMIT_BYTES),
    )(pos_flat, ys, gates_tk, x, gate2)


def _route(ids, ranks, counts, rows, n_blocks):
    padded = (counts + rows - 1) // rows * rows
    pad_end = jnp.cumsum(padded)
    pad_start = pad_end - padded
    pos = pad_start[ids] + ranks
    block_e = jnp.minimum(jnp.searchsorted(pad_end, jnp.arange(n_blocks, dtype=jnp.int32) * rows,
                                           side="right"), N_EXPERTS - 1).astype(jnp.int32)
    n_used = jnp.maximum(pad_end[-1:] // rows, 1).astype(jnp.int32)
    return pos.astype(jnp.int32), block_e, n_used


def _tiles_to_tokens(a):
    n_tiles, k, tm = a.shape
    return jnp.transpose(a, (1, 0, 2)).reshape(k, n_tiles * tm)


def _layer_weights(l, w_in, g_norm_mix, g_norm_ffn, a_q_norm, a_k_norm, b_ln_g, b_ln_b,
                   c_q_norm, c_k_norm, w_branch, w_out, w_router, b_router, widths):
    d = w_in.shape[1]
    aw, kvw = widths[0], widths[6]
    splits = np.cumsum(widths)[:-1].tolist()
    w_split = [w.astype(BF16) for w in jnp.split(w_in[l], splits, axis=-1)]
    head = np.arange(aw) // HEAD_DIM
    bd = jnp.asarray(head[:, None] == head[None, :], BF16)
    tile = lambda v, w: jnp.tile(v, w // HEAD_DIM).reshape(1, w)
    return dict(
        w_split=w_split, bd=bd,
        g_mix=g_norm_mix[l].reshape(1, d), g_ffn=g_norm_ffn[l].reshape(1, d),
        aqn=tile(a_q_norm[l], aw), akn=tile(a_k_norm[l], aw),
        cqn=tile(c_q_norm[l], aw), ckn=tile(c_k_norm[l], kvw),
        lng=b_ln_g[l].reshape(1, aw), lnb=b_ln_b[l].reshape(1, aw),
        w_br=w_branch[l].astype(BF16), w_out=w_out[l].astype(BF16),
        w_rt=jnp.transpose(w_router[l]), b_r=b_router[l].reshape(N_EXPERTS, 1))


def kernel(x_prompt, x_sample, cache_a_k, cache_a_v, cache_c_k, cache_c_v, c_prompt, c_sample,
           w_ada, b_ada, g_norm_mix, g_norm_ffn, w_in, a_q_norm, a_k_norm, a_rel_bias,
           b_ln_g, b_ln_b, b_w_s, b_b_s, c_q_norm, c_k_norm, c_sinks, t5_bias,
           w_branch, w_out, w_router, b_router, w_gate_up, b_gate_up, w_down, b_down):
    bp, sp, d = x_prompt.shape
    bs, ss, _ = x_sample.shape
    depth = w_in.shape[0]
    ts = bs * ss
    tp = bp * sp
    a_heads, c_kv_heads = cache_a_k.shape[3], cache_c_k.shape[3]
    aw = a_heads * HEAD_DIM
    kvw = c_kv_heads * HEAD_DIM
    widths = [aw, aw, aw, aw, aw, aw, kvw, kvw, N_BRANCH * d]
    ra, rc = cache_a_k.shape[2], cache_c_k.shape[2]
    a_band, c_band = A_PREV_CHUNKS * CHUNK, C_PREV_CHUNKS * CHUNK
    proj_tile = min(PROJ_TILE, sp)
    a_tile, c_tile = min(A_Q_TILE, sp), min(C_Q_TILE, sp)
    moe_tile = min(MOE_TOKEN_TILE, tp)

    c_rows = jnp.concatenate([c_prompt, c_sample,
                              jnp.zeros((ADA_ROWS - bp - bs, d), F32)], axis=0)
    mod_all = _ada_modulation(c_rows, w_ada, b_ada).reshape(depth, ADA_ROWS, 6, d)

    t5_p = _c_bias_table(t5_bias, c_band + jnp.arange(CHUNK), jnp.arange(c_band + CHUNK))
    t5_s = _c_bias_table(t5_bias, rc + jnp.arange(ss), jnp.arange(rc + ss))
    bias_c_p = _band_bias(t5_p, c_tile, C_PREV_CHUNKS)

    n_assign = (tp + ts) * TOP_K
    n_blocks = -(-(n_assign + N_EXPERTS * (MOE_ROWS - 1)) // MOE_ROWS)
    n_rows = n_blocks * MOE_ROWS

    mask = (np.arange(B_CHUNK)[None, :] // CHUNK <= np.arange(B_CHUNK)[:, None] // CHUNK)
    mask = jnp.asarray(mask, F32)

    xp = x_prompt
    xs = x_sample.reshape(1, ts, d)
    st_p, st_s = [], []
    for l in range(depth):
        lw = _layer_weights(l, w_in, g_norm_mix, g_norm_ffn, a_q_norm, a_k_norm, b_ln_g, b_ln_b,
                            c_q_norm, c_k_norm, w_branch, w_out, w_router, b_router, widths)
        mod_p = mod_all[l, :bp]
        mod_s = jnp.repeat(mod_all[l, bp:bp + bs], ss, axis=0)
        mp = [mod_p[:, i].reshape(bp, 1, d) for i in range(6)]
        ms = [mod_s[:, i].reshape(1, ts, d) for i in range(6)]

        ws_m = b_w_s[l] * mask
        sgu_w_p = ws_m.astype(BF16)
        sgu_b_p = jnp.repeat(jnp.transpose(b_b_s[l]), aw // B_GROUPS, axis=1)
        eye = jnp.eye(B_CHUNK // ss, dtype=F32)
        sgu_w_s = jnp.stack([jnp.kron(eye, ws_m[g, :ss, :ss]) for g in range(B_GROUPS)]).astype(BF16)
        sgu_b_s = jnp.tile(sgu_b_p[:ss], (B_CHUNK // ss, 1))

        bias_a_p = _band_bias(
            _a_bias_table(a_rel_bias[l], a_band + jnp.arange(CHUNK), jnp.arange(a_band + CHUNK)),
            a_tile, A_PREV_CHUNKS)
        bias_a_s = _a_bias_table(a_rel_bias[l], ra + jnp.arange(ss), jnp.arange(ra + ss))
        sinks = c_sinks[l].astype(F32)

        aq, ak, av, bu, bv, cq, ck, cv, g = _project(xp, mp[0], mp[1], lw, proj_tile)
        oa = _band_attention(aq, ak, av, bias_a_p, None, a_band, a_tile)
        oc = _band_attention(cq, ck, cv, bias_c_p, sinks, c_band, c_tile)
        na, nc = min(a_band, sp), min(c_band, sp)
        st_p.append((ak[:, sp - na:].reshape(bp, na, a_heads, HEAD_DIM),
                     av[:, sp - na:].reshape(bp, na, a_heads, HEAD_DIM),
                     ck[:, sp - nc:].reshape(bp, nc, c_kv_heads, HEAD_DIM),
                     cv[:, sp - nc:].reshape(bp, nc, c_kv_heads, HEAD_DIM)))
        zero_counts = jnp.zeros((N_EXPERTS, 128), F32)
        xp_mid, h2_p, ids_p, gates_p, rank_p, counts_p = _merge(
            oa, oc, bu, bv, g, xp, mp[2], mp[3], mp[4], lw, sgu_w_p, sgu_b_p, zero_counts,
            proj_tile)

        aq, ak, av, bu, bv, cq, ck, cv, g = _project(xs, ms[0], ms[1], lw, ts)
        per_seq = lambda a: a.reshape(bs, ss, a.shape[-1])
        oa = _band_attention(per_seq(aq), per_seq(ak), per_seq(av), bias_a_s, None, ra, ss,
                             prev=(cache_a_k[l].reshape(bs, ra, aw), cache_a_v[l].reshape(bs, ra, aw)))
        oc = _band_attention(per_seq(cq), per_seq(ck), per_seq(cv), t5_s, sinks, rc, ss,
                             prev=(cache_c_k[l].reshape(bs, rc, kvw), cache_c_v[l].reshape(bs, rc, kvw)))
        st_s.append((ak.reshape(bs, ss, a_heads, HEAD_DIM), av.reshape(bs, ss, a_heads, HEAD_DIM),
                     ck.reshape(bs, ss, c_kv_heads, HEAD_DIM), cv.reshape(bs, ss, c_kv_heads, HEAD_DIM),
                     bv.reshape(bs, ss, aw)))
        xs_mid, h2_s, ids_s, gates_s, rank_s, counts = _merge(
            oa.reshape(1, ts, aw), oc.reshape(1, ts, aw), bu, bv, g, xs, ms[2], ms[3], ms[4], lw,
            sgu_w_s, sgu_b_s, counts_p, ts)

        ids = jnp.concatenate([_tiles_to_tokens(ids_p), _tiles_to_tokens(ids_s)], axis=1)
        ranks = jnp.concatenate([_tiles_to_tokens(rank_p), _tiles_to_tokens(rank_s)], axis=1)
        pos, block_e, n_used = _route(ids, ranks, counts[:, 0].astype(jnp.int32), MOE_ROWS, n_blocks)
        pos_tk = jnp.transpose(pos)
        pos_p = pos_tk[:tp].reshape(-1)
        pos_s = pos_tk[tp:].reshape(-1)
        rows_in = jnp.zeros((n_rows, d), F32)
        rows_in = _dispatch(pos_p, h2_p.reshape(tp, d), rows_in, moe_tile)
        rows_in = _dispatch(pos_s, h2_s.reshape(ts, d), rows_in, ts)
        rows_out = _experts(block_e, n_used, rows_in, w_gate_up[l], b_gate_up[l], w_down[l],
                            b_down[l], MOE_ROWS)
        gates_p_tk = jnp.transpose(_tiles_to_tokens(gates_p)).reshape(bp, sp, TOP_K)
        gates_s_tk = jnp.transpose(_tiles_to_tokens(gates_s)).reshape(1, ts, TOP_K)
        xp = _combine(pos_p, rows_out, gates_p_tk, xp_mid, mp[5], moe_tile)
        xs = _combine(pos_s, rows_out, gates_s_tk, xs_mid, ms[5], ts)

    stack = lambda rows, i: jnp.stack([r[i] for r in rows], axis=0)
    return (xp, xs.reshape(bs, ss, d),
            stack(st_p, 0), stack(st_p, 1), stack(st_p, 2), stack(st_p, 3),
            stack(st_s, 0), stack(st_s, 1), stack(st_s, 2), stack(st_s, 3), stack(st_s, 4))
```

```python
import functools
import math

import jax
import jax.numpy as jnp
import numpy as np
from jax import lax
from jax.experimental import pallas as pl
from jax.experimental.pallas import tpu as pltpu

F32 = jnp.float32
BF16 = jnp.bfloat16

CHUNK = 64
HEAD_DIM = 64
A_PREV_CHUNKS = 8
A_REL_CLIP = 128
B_CHUNK = 128
B_GROUPS = 4
C_PREV_CHUNKS = 2
T5_BUCKETS = 32
T5_MAX_DIST = 128
N_BRANCH = 3
N_EXPERTS = 32
TOP_K = 4
SWIGLU_ALPHA = 1.702
SWIGLU_LIMIT = 7.0
RMS_EPS = 1e-6
LN_EPS = 1e-5
NEG_INF = -1e30
F32_LOWEST = float(np.finfo(np.float32).min)

VMEM_LIMIT_BYTES = 56 * 1024 * 1024
ADA_ROWS = 16
PROJ_TILE = 256
A_Q_TILE = 256
C_Q_TILE = 128
MOE_ROWS = 256
MOE_TOKEN_TILE = 256


def _cparams(n_axes, vmem=None):
    return pltpu.CompilerParams(dimension_semantics=("arbitrary",) * n_axes,
                                vmem_limit_bytes=vmem)


def _const_spec(shape):
    nd = len(shape)
    return pl.BlockSpec(shape, lambda *_: (0,) * nd)


def _ada_kernel(c_ref, w_ref, b_ref, o_ref):
    c = c_ref[...]
    s = c * jax.nn.sigmoid(c)
    o_ref[0] = jnp.dot(s, w_ref[0], preferred_element_type=F32,
                       precision=lax.Precision.HIGHEST) + b_ref[0]


def _ada_modulation(c_rows, w_ada, b_ada):
    depth, d, _ = w_ada.shape
    return pl.pallas_call(
        _ada_kernel,
        grid=(depth, 6),
        in_specs=[pl.BlockSpec((ADA_ROWS, d), lambda l, j: (0, 0)),
                  pl.BlockSpec((1, d, d), lambda l, j: (l, 0, j)),
                  pl.BlockSpec((1, 1, d), lambda l, j: (l, 0, j))],
        out_specs=pl.BlockSpec((1, ADA_ROWS, d), lambda l, j: (l, 0, j)),
        out_shape=jax.ShapeDtypeStruct((depth, ADA_ROWS, 6 * d), F32),
        compiler_params=_cparams(2),
    )(c_rows, w_ada, b_ada.reshape(depth, 1, 6 * d))


def _row_rms(x, gain):
    return x * lax.rsqrt(jnp.mean(x * x, axis=-1, keepdims=True) + RMS_EPS) * gain


def _head_rms(t, gain, bd):
    sq = t * t
    hi = sq.astype(BF16)
    lo = (sq - hi.astype(F32)).astype(BF16)
    ss = (jnp.dot(hi, bd, preferred_element_type=F32)
          + jnp.dot(lo, bd, preferred_element_type=F32))
    return t * lax.rsqrt(ss * (1.0 / HEAD_DIM) + RMS_EPS) * gain


def _proj_kernel(x_ref, shift_ref, scale_ref, gmix_ref,
                 w_aq, w_ak, w_av, w_bu, w_bv, w_cq, w_ck, w_cv, w_g,
                 bd_ref, aqn, akn, cqn, ckn, lng, lnb,
                 aq_o, ak_o, av_o, bu_o, bv_o, cq_o, ck_o, cv_o, g_o):
    x = x_ref[0]
    h = _row_rms(x, gmix_ref[...]) * (1.0 + scale_ref[0]) + shift_ref[0]
    hb = h.astype(BF16)

    def mm(w_ref):
        return jnp.dot(hb, w_ref[...], preferred_element_type=F32)

    bd = bd_ref[...]
    q_scale = HEAD_DIM ** -0.5
    aq_o[0] = (_head_rms(mm(w_aq), aqn[...], bd) * q_scale).astype(BF16)
    ak_o[0] = _head_rms(mm(w_ak), akn[...], bd)
    av_o[0] = mm(w_av)
    bu_o[0] = jax.nn.gelu(mm(w_bu)).astype(BF16)
    bv = jax.nn.gelu(mm(w_bv))
    mu = jnp.mean(bv, axis=-1, keepdims=True)
    var = jnp.mean(jnp.square(bv - mu), axis=-1, keepdims=True)
    bv_o[0] = (bv - mu) * lax.rsqrt(var + LN_EPS) * lng[...] + lnb[...]
    cq_o[0] = (_head_rms(mm(w_cq), cqn[...], bd) * q_scale).astype(BF16)
    kvw = ck_o.shape[-1]
    ck_o[0] = _head_rms(mm(w_ck), ckn[...], bd[:kvw, :kvw])
    cv_o[0] = mm(w_cv)
    d = x.shape[-1]
    for r in range(N_BRANCH):
        g_o[0, :, r * d:(r + 1) * d] = jax.nn.sigmoid(
            jnp.dot(hb, w_g[:, r * d:(r + 1) * d], preferred_element_type=F32)).astype(BF16)


def _project(x, shift, scale, lw, tm):
    b, s, d = x.shape
    r = shift.shape[1]
    widths = [w.shape[1] for w in lw["w_split"]]
    aw, kvw = widths[0], widths[6]
    tok = lambda w: pl.BlockSpec((1, tm, w), lambda bi, i: (bi, i, 0))
    mod = pl.BlockSpec((1, r, d), lambda bi, i: (bi, 0, 0))
    in_specs = ([tok(d), mod, mod, _const_spec((1, d))]
                + [_const_spec(w.shape) for w in lw["w_split"]]
                + [_const_spec(lw["bd"].shape)]
                + [_const_spec((1, aw)), _const_spec((1, aw)), _const_spec((1, aw)),
                   _const_spec((1, kvw)), _const_spec((1, aw)), _const_spec((1, aw))])
    out_dt = [BF16, F32, F32, BF16, F32, BF16, F32, F32, BF16]
    return pl.pallas_call(
        _proj_kernel,
        grid=(b, s // tm),
        in_specs=in_specs,
        out_specs=[tok(w) for w in widths],
        out_shape=[jax.ShapeDtypeStruct((b, s, w), dt) for w, dt in zip(widths, out_dt)],
        compiler_params=_cparams(2, VMEM_LIMIT_BYTES),
    )(x, shift, scale, lw["g_mix"], *lw["w_split"], lw["bd"],
      lw["aqn"], lw["akn"], lw["cqn"], lw["ckn"], lw["lng"], lw["lnb"])


def _attn_kernel(*refs, n_q_heads, n_kv_heads, has_sink, carry):
    refs = list(refs)
    q_ref, k_ref, v_ref = refs[:3]
    pos = 3
    if not carry:
        pk_ref, pv_ref = refs[pos:pos + 2]
        pos += 2
    bias_ref = refs[pos]
    pos += 1
    if has_sink:
        sink_ref = refs[pos]
        pos += 1
    o_ref = refs[pos]
    pos += 1
    if carry:
        pk_scr, pv_scr = refs[pos:pos + 2]

    j = pl.program_id(1)
    kc = k_ref[0].astype(BF16)
    vc = v_ref[0].astype(BF16)
    tq = kc.shape[0]
    if carry:
        @pl.when(j == 0)
        def _():
            pk_scr[...] = jnp.zeros_like(pk_scr)
            pv_scr[...] = jnp.zeros_like(pv_scr)
        kp, vp = pk_scr[...], pv_scr[...]
    else:
        kp, vp = pk_ref[0].astype(BF16), pv_ref[0].astype(BF16)
    n_prev = kp.shape[0]
    kw = jnp.concatenate([kp, kc], axis=0)
    vw = jnp.concatenate([vp, vc], axis=0)
    width = n_prev + tq
    if carry:
        before_start = lax.broadcasted_iota(jnp.int32, (tq, width), 1) < (n_prev - j * tq)
    q = q_ref[0]
    group = n_q_heads // n_kv_heads
    outs = []
    for h in range(n_q_heads):
        kh = h // group
        qh = q[:, h * HEAD_DIM:(h + 1) * HEAD_DIM]
        s = lax.dot_general(qh, kw[:, kh * HEAD_DIM:(kh + 1) * HEAD_DIM],
                            (((1,), (1,)), ((), ())), preferred_element_type=F32) + bias_ref[h]
        if carry:
            s = jnp.where(before_start, NEG_INF, s)
        m = jnp.max(s, axis=-1, keepdims=True)
        if has_sink:
            sink = sink_ref[h]
            m = jnp.maximum(m, sink)
        e = jnp.exp(s - m)
        denom = jnp.sum(e, axis=-1, keepdims=True)
        if has_sink:
            denom = denom + jnp.exp(sink - m)
        o = jnp.dot(e.astype(BF16), vw[:, kh * HEAD_DIM:(kh + 1) * HEAD_DIM],
                    preferred_element_type=F32)
        outs.append(o / denom)
    o_ref[0] = jnp.concatenate(outs, axis=-1).astype(BF16)
    if carry:
        pk_scr[...] = kw[tq:, :]
        pv_scr[...] = vw[tq:, :]


def _band_attention(q, k, v, bias, sinks, n_prev_rows, tq, prev=None):
    b, s, qw = q.shape
    kvw = k.shape[-1]
    hq, kvh = qw // HEAD_DIM, kvw // HEAD_DIM
    carry = prev is None
    width = n_prev_rows + tq
    tok = lambda w: pl.BlockSpec((1, tq, w), lambda bi, i: (bi, i, 0))
    in_specs = [tok(qw), tok(kvw), tok(kvw)]
    args = [q, k, v]
    if not carry:
        pspec = pl.BlockSpec((1, n_prev_rows, kvw), lambda bi, i: (bi, 0, 0))
        in_specs += [pspec, pspec]
        args += list(prev)
    in_specs.append(_const_spec((hq, tq, width)))
    args.append(bias)
    if sinks is not None:
        in_specs.append(pl.BlockSpec(memory_space=pltpu.SMEM))
        args.append(sinks)
    scratch = [pltpu.VMEM((n_prev_rows, kvw), BF16)] * 2 if carry else []
    return pl.pallas_call(
        functools.partial(_attn_kernel, n_q_heads=hq, n_kv_heads=kvh,
                          has_sink=sinks is not None, carry=carry),
        grid=(b, s // tq),
        in_specs=in_specs,
        out_specs=tok(qw),
        out_shape=jax.ShapeDtypeStruct((b, s, qw), BF16),
        scratch_shapes=scratch,
        compiler_params=_cparams(2, VMEM_LIMIT_BYTES),
    )(*args)


def _band_bias(base, tq, n_prev_chunks):
    n_qc = tq // CHUNK
    blocks = [jnp.pad(base, ((0, 0), (0, 0), (jq * CHUNK, (n_qc - 1 - jq) * CHUNK)),
                      constant_values=NEG_INF) for jq in range(n_qc)]
    return jnp.concatenate(blocks, axis=1)


def _toeplitz(vec, n_q, n_k):
    return jnp.stack([vec[:, n_q - 1 - i:n_q - 1 - i + n_k] for i in range(n_q)], axis=1)


def _a_bias_table(a_rel, q0, n_q, n_k):
    t = np.arange(n_q + n_k - 1)
    idx = np.clip(q0 + n_q - 1 - t, -A_REL_CLIP, A_REL_CLIP) + A_REL_CLIP
    return _toeplitz(a_rel[:, idx].astype(F32), n_q, n_k)


def _t5_bucket(rel):
    n = T5_BUCKETS // 2
    max_exact = n // 2
    ret = jnp.where(rel > 0, n, 0)
    a = jnp.abs(rel)
    large = max_exact + (jnp.log(jnp.maximum(a, 1).astype(F32) / max_exact)
                         / math.log(T5_MAX_DIST / max_exact) * (n - max_exact)).astype(jnp.int32)
    large = jnp.minimum(large, n - 1)
    return ret + jnp.where(a < max_exact, a, large)


def _c_bias_table(t5_table, q0, n_q, n_k):
    rel = jnp.arange(n_q + n_k - 1, dtype=jnp.int32) - (n_q - 1) - q0
    vec = jnp.transpose(t5_table[_t5_bucket(rel)]).astype(F32)
    return _toeplitz(vec, n_q, n_k)


def _merge_kernel(oa_ref, oc_ref, bu_ref, bv_ref, g_ref, x_ref, gate1_ref, shift2_ref, scale2_ref,
                  gffn_ref, wbr_ref, wout_ref, ws_ref, bsb_ref, wrt_ref, br_ref, tri_ref, cnt_ref,
                  xnew_o, h2_o, ids_o, gates_o, rank_o, cnt_o, base_scr):
    first = jnp.logical_and(pl.program_id(0) == 0, pl.program_id(1) == 0)

    @pl.when(first)
    def _():
        base_scr[...] = cnt_ref[...]

    tm, d = x_ref.shape[1], x_ref.shape[2]
    bw = bv_ref.shape[2]
    gch = bw // B_GROUPS
    bvb = bv_ref[0].astype(BF16)
    rows = []
    for r in range(tm // B_CHUNK):
        cols = [jnp.dot(ws_ref[gi], bvb[r * B_CHUNK:(r + 1) * B_CHUNK, gi * gch:(gi + 1) * gch],
                        preferred_element_type=F32) for gi in range(B_GROUPS)]
        rows.append(jnp.concatenate(cols, axis=-1) + bsb_ref[...])
    mixed = jnp.concatenate(rows, axis=0) if len(rows) > 1 else rows[0]
    ob = (bu_ref[0].astype(F32) * mixed).astype(BF16)

    branches = (oa_ref[0], ob, oc_ref[0])
    y = None
    for r in range(N_BRANCH):
        br = jnp.dot(branches[r], wbr_ref[r], preferred_element_type=F32)
        term = g_ref[0, :, r * d:(r + 1) * d].astype(F32) * br
        y = term if y is None else y + term
    out = jnp.dot(y.astype(BF16), wout_ref[...], preferred_element_type=F32)
    xn = x_ref[0] + gate1_ref[0] * out
    xnew_o[0] = xn
    h2 = _row_rms(xn, gffn_ref[...]) * (1.0 + scale2_ref[0]) + shift2_ref[0]
    h2_o[0] = h2

    logits = lax.dot_general(wrt_ref[...], h2, (((1,), (1,)), ((), ())),
                             preferred_element_type=F32,
                             precision=lax.Precision.HIGHEST) + br_ref[...]
    n_exp = logits.shape[0]
    eidx = lax.broadcasted_iota(jnp.int32, (n_exp, tm), 0).astype(F32)
    vals, ids = [], []
    rest = logits
    for _ in range(TOP_K):
        m = jnp.max(rest, axis=0, keepdims=True)
        sel = jnp.min(jnp.where(rest == m, eidx, float(n_exp)), axis=0, keepdims=True)
        vals.append(m)
        ids.append(sel)
        rest = jnp.where(eidx == sel, F32_LOWEST, rest)
    exps = [jnp.exp(v - vals[0]) for v in vals]
    tot = exps[0] + exps[1] + exps[2] + exps[3]
    gates_o[0] = jnp.concatenate([e / tot for e in exps], axis=0)
    ids_o[0] = jnp.concatenate(ids, axis=0).astype(jnp.int32)

    base = base_scr[...]
    ranks = []
    for k in range(TOP_K):
        hit = eidx == ids[k]
        onehot = jnp.where(hit, 1.0, 0.0)
        before = jnp.dot(onehot.astype(BF16), tri_ref[...], preferred_element_type=F32)
        ranks.append(jnp.sum(jnp.where(hit, base[:, 0:1] + before, 0.0), axis=0, keepdims=True))
        base = base + jnp.sum(onehot, axis=1, keepdims=True)
    rank_o[0] = jnp.concatenate(ranks, axis=0).astype(jnp.int32)
    base_scr[...] = base
    cnt_o[...] = base


def _merge(oa, oc, bu, bv, g, x, gate1, shift2, scale2, lw, sgu_w, sgu_b, counts, tm):
    b, s, d = x.shape
    r = gate1.shape[1]
    bw = bu.shape[-1]
    n_tiles = b * (s // tm)
    tok = lambda w: pl.BlockSpec((1, tm, w), lambda bi, i: (bi, i, 0))
    mod = pl.BlockSpec((1, r, d), lambda bi, i: (bi, 0, 0))
    per_tile = pl.BlockSpec((1, TOP_K, tm), lambda bi, i: (bi * (s // tm) + i, 0, 0))
    tri = jnp.asarray(np.triu(np.ones((tm, tm), np.float32), k=1), BF16)
    in_specs = [tok(bw), tok(bw), tok(bw), tok(bw), tok(N_BRANCH * d), tok(d), mod, mod, mod,
                _const_spec((1, d)), _const_spec(lw["w_br"].shape), _const_spec((d, d)),
                _const_spec(sgu_w.shape), _const_spec(sgu_b.shape),
                _const_spec((N_EXPERTS, d)), _const_spec((N_EXPERTS, 1)),
                _const_spec((tm, tm)), _const_spec((N_EXPERTS, 128))]
    out_shape = [jax.ShapeDtypeStruct((b, s, d), F32), jax.ShapeDtypeStruct((b, s, d), F32),
                 jax.ShapeDtypeStruct((n_tiles, TOP_K, tm), jnp.int32),
                 jax.ShapeDtypeStruct((n_tiles, TOP_K, tm), F32),
                 jax.ShapeDtypeStruct((n_tiles, TOP_K, tm), jnp.int32),
                 jax.ShapeDtypeStruct((N_EXPERTS, 128), F32)]
    out_specs = [tok(d), tok(d), per_tile, per_tile, per_tile, _const_spec((N_EXPERTS, 128))]
    return pl.pallas_call(
        _merge_kernel,
        grid=(b, s // tm),
        in_specs=in_specs,
        out_specs=out_specs,
        out_shape=out_shape,
        scratch_shapes=[pltpu.VMEM((N_EXPERTS, 128), F32)],
        compiler_params=_cparams(2, VMEM_LIMIT_BYTES),
    )(oa, oc, bu, bv, g, x, gate1, shift2, scale2, lw["g_ffn"], lw["w_br"], lw["w_out"],
      sgu_w, sgu_b, lw["w_rt"], lw["b_r"], tri, counts)


def _row_copy(src_ref, src_row, dst_ref, dst_row, sem):
    return pltpu.make_async_copy(src_ref.at[pl.ds(src_row, 1)], dst_ref.at[pl.ds(dst_row, 1)], sem)


def _dispatch_kernel(pos_ref, h_ref, xs_in_ref, xs_ref, sem):
    del xs_in_ref
    tm = h_ref.shape[0]
    base = pl.program_id(0) * (tm * TOP_K)

    def issue(t, carry):
        for k in range(TOP_K):
            _row_copy(h_ref, t, xs_ref, pos_ref[base + t * TOP_K + k], sem).start()
        return carry

    lax.fori_loop(0, tm, issue, 0)

    def drain(t, carry):
        for k in range(TOP_K):
            _row_copy(h_ref, 0, xs_ref, 0, sem).wait()
        return carry

    lax.fori_loop(0, tm, drain, 0)


def _dispatch(pos_flat, h, xs, tm):
    t, d = h.shape
    return pl.pallas_call(
        _dispatch_kernel,
        grid_spec=pltpu.PrefetchScalarGridSpec(
            num_scalar_prefetch=1,
            grid=(t // tm,),
            in_specs=[pl.BlockSpec((tm, d), lambda i, pos: (i, 0)),
                      pl.BlockSpec(memory_space=pl.ANY)],
            out_specs=pl.BlockSpec(memory_space=pl.ANY),
            scratch_shapes=[pltpu.SemaphoreType.DMA]),
        out_shape=jax.ShapeDtypeStruct(xs.shape, xs.dtype),
        input_output_aliases={2: 0},
        compiler_params=_cparams(1),
    )(pos_flat, h, xs)


def _expert_kernel(be_ref, nused_ref, xs_ref, wgu_ref, bgu_ref, wdn_ref, bdn_ref, ys_ref,
                   wgu_scr, wdn_scr):
    i = pl.program_id(0)
    prev_e = be_ref[jnp.maximum(i - 1, 0)]
    new_expert = jnp.logical_or(i == 0, be_ref[i] != prev_e)
    active = i < nused_ref[0]

    @pl.when(jnp.logical_and(active, new_expert))
    def _():
        wgu_scr[...] = wgu_ref[0].astype(BF16)
        wdn_scr[...] = wdn_ref[0].astype(BF16)

    @pl.when(active)
    def _():
        de = wdn_scr.shape[0]
        gu = jnp.dot(xs_ref[...].astype(BF16), wgu_scr[...],
                     preferred_element_type=F32) + bgu_ref[0]
        g = jnp.minimum(gu[:, :de], SWIGLU_LIMIT)
        u = jnp.clip(gu[:, de:], -SWIGLU_LIMIT, SWIGLU_LIMIT)
        act = (u + 1.0) * g * jax.nn.sigmoid(SWIGLU_ALPHA * g)
        ys_ref[...] = jnp.dot(act.astype(BF16), wdn_scr[...],
                              preferred_element_type=F32) + bdn_ref[0]

    @pl.when(jnp.logical_not(active))
    def _():
        ys_ref[...] = jnp.zeros_like(ys_ref)


def _experts(block_e, n_used, xs, w_gu, b_gu, w_dn, b_dn, rows):
    n_rows, d = xs.shape
    n_e, _, two_de = w_gu.shape
    de = two_de // 2
    n_blocks = n_rows // rows
    row_map = lambda i, be, nu: (jnp.minimum(i, nu[0] - 1), 0)
    e_map = lambda i, be, nu: (be[jnp.minimum(i, nu[0] - 1)], 0, 0)
    return pl.pallas_call(
        _expert_kernel,
        grid_spec=pltpu.PrefetchScalarGridSpec(
            num_scalar_prefetch=2,
            grid=(n_blocks,),
            in_specs=[pl.BlockSpec((rows, d), row_map),
                      pl.BlockSpec((1, d, two_de), e_map),
                      pl.BlockSpec((1, 1, two_de), e_map),
                      pl.BlockSpec((1, de, d), e_map),
                      pl.BlockSpec((1, 1, d), e_map)],
            out_specs=pl.BlockSpec((rows, d), lambda i, be, nu: (i, 0)),
            scratch_shapes=[pltpu.VMEM((d, two_de), BF16), pltpu.VMEM((de, d), BF16)]),
        out_shape=jax.ShapeDtypeStruct((n_rows, d), F32),
        compiler_params=_cparams(1, VMEM_LIMIT_BYTES),
    )(block_e, n_used, xs, w_gu, b_gu.reshape(n_e, 1, two_de), w_dn, b_dn.reshape(n_e, 1, d))


def _combine_kernel(pos_ref, ys_ref, gates_ref, x_ref, gate2_ref, o_ref, buf, sem):
    tm = x_ref.shape[1]
    n_i = pl.num_programs(1)
    base = (pl.program_id(0) * n_i + pl.program_id(1)) * (tm * TOP_K)

    def issue(t, carry):
        for k in range(TOP_K):
            _row_copy(ys_ref, pos_ref[base + t * TOP_K + k], buf.at[k], t, sem).start()
        return carry

    lax.fori_loop(0, tm, issue, 0)

    def drain(t, carry):
        for k in range(TOP_K):
            _row_copy(ys_ref, 0, buf.at[k], 0, sem).wait()
        return carry

    lax.fori_loop(0, tm, drain, 0)

    gates = gates_ref[0]
    ff = gates[:, 0:1] * buf[0]
    for k in range(1, TOP_K):
        ff = ff + gates[:, k:k + 1] * buf[k]
    o_ref[0] = x_ref[0] + gate2_ref[0] * ff


def _combine(pos_flat, ys, gates_tk, x, gate2, tm):
    b, s, d = x.shape
    r = gate2.shape[1]
    return pl.pallas_call(
        _combine_kernel,
        grid_spec=pltpu.PrefetchScalarGridSpec(
            num_scalar_prefetch=1,
            grid=(b, s // tm),
            in_specs=[pl.BlockSpec(memory_space=pl.ANY),
                      pl.BlockSpec((1, tm, TOP_K), lambda bi, i, pos: (bi, i, 0)),
                      pl.BlockSpec((1, tm, d), lambda bi, i, pos: (bi, i, 0)),
                      pl.BlockSpec((1, r, d), lambda bi, i, pos: (bi, 0, 0))],
            out_specs=pl.BlockSpec((1, tm, d), lambda bi, i, pos: (bi, i, 0)),
            scratch_shapes=[pltpu.VMEM((TOP_K, tm, d), F32), pltpu.SemaphoreType.DMA]),
        out_shape=jax.ShapeDtypeStruct((b, s, d), F32),
        compiler_params=_cparams(2, VMEM_LIMIT_BYTES),
    )(pos_flat, ys, gates_tk, x, gate2)


def _route(ids, ranks, counts, rows, n_blocks):
    padded = (counts + rows - 1) // rows * rows
    pad_end = jnp.cumsum(padded)
    pad_start = pad_end - padded
    experts = jnp.arange(N_EXPERTS, dtype=jnp.int32)
    pos = jnp.sum(jnp.where(ids[..., None] == experts, pad_start, 0), axis=-1) + ranks
    first_row = jnp.arange(n_blocks, dtype=jnp.int32) * rows
    block_e = jnp.minimum(jnp.sum((pad_end[None, :] <= first_row[:, None]).astype(jnp.int32), axis=1),
                          N_EXPERTS - 1)
    n_used = jnp.maximum(pad_end[-1:] // rows, 1).astype(jnp.int32)
    return pos.astype(jnp.int32), block_e, n_used


def _tiles_to_tokens(a):
    n_tiles, k, tm = a.shape
    return jnp.transpose(a, (1, 0, 2)).reshape(k, n_tiles * tm)


def _layer_weights(l, w_in, g_norm_mix, g_norm_ffn, a_q_norm, a_k_norm, b_ln_g, b_ln_b,
                   c_q_norm, c_k_norm, w_branch, w_out, w_router, b_router, widths):
    d = w_in.shape[1]
    aw, kvw = widths[0], widths[6]
    splits = np.cumsum(widths)[:-1].tolist()
    w_split = [w.astype(BF16) for w in jnp.split(w_in[l], splits, axis=-1)]
    head = np.arange(aw) // HEAD_DIM
    bd = jnp.asarray(head[:, None] == head[None, :], BF16)
    tile = lambda v, w: jnp.tile(v, w // HEAD_DIM).reshape(1, w)
    return dict(
        w_split=w_split, bd=bd,
        g_mix=g_norm_mix[l].reshape(1, d), g_ffn=g_norm_ffn[l].reshape(1, d),
        aqn=tile(a_q_norm[l], aw), akn=tile(a_k_norm[l], aw),
        cqn=tile(c_q_norm[l], aw), ckn=tile(c_k_norm[l], kvw),
        lng=b_ln_g[l].reshape(1, aw), lnb=b_ln_b[l].reshape(1, aw),
        w_br=w_branch[l].astype(BF16), w_out=w_out[l].astype(BF16),
        w_rt=jnp.transpose(w_router[l]), b_r=b_router[l].reshape(N_EXPERTS, 1))


def kernel(x_prompt, x_sample, cache_a_k, cache_a_v, cache_c_k, cache_c_v, c_prompt, c_sample,
           w_ada, b_ada, g_norm_mix, g_norm_ffn, w_in, a_q_norm, a_k_norm, a_rel_bias,
           b_ln_g, b_ln_b, b_w_s, b_b_s, c_q_norm, c_k_norm, c_sinks, t5_bias,
           w_branch, w_out, w_router, b_router, w_gate_up, b_gate_up, w_down, b_down):
    bp, sp, d = x_prompt.shape
    bs, ss, _ = x_sample.shape
    depth = w_in.shape[0]
    ts = bs * ss
    tp = bp * sp
    a_heads, c_kv_heads = cache_a_k.shape[3], cache_c_k.shape[3]
    aw = a_heads * HEAD_DIM
    kvw = c_kv_heads * HEAD_DIM
    widths = [aw, aw, aw, aw, aw, aw, kvw, kvw, N_BRANCH * d]
    ra, rc = cache_a_k.shape[2], cache_c_k.shape[2]
    a_band, c_band = A_PREV_CHUNKS * CHUNK, C_PREV_CHUNKS * CHUNK
    proj_tile = min(PROJ_TILE, sp)
    a_tile, c_tile = min(A_Q_TILE, sp), min(C_Q_TILE, sp)
    moe_tile = min(MOE_TOKEN_TILE, tp)

    c_rows = jnp.concatenate([c_prompt, c_sample,
                              jnp.zeros((ADA_ROWS - bp - bs, d), F32)], axis=0)
    mod_all = _ada_modulation(c_rows, w_ada, b_ada).reshape(depth, ADA_ROWS, 6, d)

    t5_p = _c_bias_table(t5_bias, c_band, CHUNK, c_band + CHUNK)
    t5_s = _c_bias_table(t5_bias, rc, ss, rc + ss)
    bias_c_p = _band_bias(t5_p, c_tile, C_PREV_CHUNKS)

    n_assign = (tp + ts) * TOP_K
    n_blocks = -(-(n_assign + N_EXPERTS * (MOE_ROWS - 1)) // MOE_ROWS)
    n_rows = n_blocks * MOE_ROWS

    mask = (np.arange(B_CHUNK)[None, :] // CHUNK <= np.arange(B_CHUNK)[:, None] // CHUNK)
    mask = jnp.asarray(mask, F32)

    xp = x_prompt
    xs = x_sample.reshape(1, ts, d)
    st_p, st_s = [], []
    for l in range(depth):
        lw = _layer_weights(l, w_in, g_norm_mix, g_norm_ffn, a_q_norm, a_k_norm, b_ln_g, b_ln_b,
                            c_q_norm, c_k_norm, w_branch, w_out, w_router, b_router, widths)
        mod_p = mod_all[l, :bp]
        mod_s = jnp.repeat(mod_all[l, bp:bp + bs], ss, axis=0)
        mp = [mod_p[:, i].reshape(bp, 1, d) for i in range(6)]
        ms = [mod_s[:, i].reshape(1, ts, d) for i in range(6)]

        ws_m = b_w_s[l] * mask
        sgu_w_p = ws_m.astype(BF16)
        sgu_b_p = jnp.repeat(jnp.transpose(b_b_s[l]), aw // B_GROUPS, axis=1)
        eye = jnp.eye(B_CHUNK // ss, dtype=F32)
        sgu_w_s = jnp.stack([jnp.kron(eye, ws_m[g, :ss, :ss]) for g in range(B_GROUPS)]).astype(BF16)
        sgu_b_s = jnp.tile(sgu_b_p[:ss], (B_CHUNK // ss, 1))

        bias_a_p = _band_bias(_a_bias_table(a_rel_bias[l], a_band, CHUNK, a_band + CHUNK),
                              a_tile, A_PREV_CHUNKS)
        bias_a_s = _a_bias_table(a_rel_bias[l], ra, ss, ra + ss)
        sinks = c_sinks[l].astype(F32)

        aq, ak, av, bu, bv, cq, ck, cv, g = _project(xp, mp[0], mp[1], lw, proj_tile)
        oa = _band_attention(aq, ak, av, bias_a_p, None, a_band, a_tile)
        oc = _band_attention(cq, ck, cv, bias_c_p, sinks, c_band, c_tile)
        na, nc = min(a_band, sp), min(c_band, sp)
        st_p.append((ak[:, sp - na:].reshape(bp, na, a_heads, HEAD_DIM),
                     av[:, sp - na:].reshape(bp, na, a_heads, HEAD_DIM),
                     ck[:, sp - nc:].reshape(bp, nc, c_kv_heads, HEAD_DIM),
                     cv[:, sp - nc:].reshape(bp, nc, c_kv_heads, HEAD_DIM)))
        zero_counts = jnp.zeros((N_EXPERTS, 128), F32)
        xp_mid, h2_p, ids_p, gates_p, rank_p, counts_p = _merge(
            oa, oc, bu, bv, g, xp, mp[2], mp[3], mp[4], lw, sgu_w_p, sgu_b_p, zero_counts,
            proj_tile)

        aq, ak, av, bu, bv, cq, ck, cv, g = _project(xs, ms[0], ms[1], lw, ts)
        per_seq = lambda a: a.reshape(bs, ss, a.shape[-1])
        oa = _band_attention(per_seq(aq), per_seq(ak), per_seq(av), bias_a_s, None, ra, ss,
                             prev=(cache_a_k[l].reshape(bs, ra, aw), cache_a_v[l].reshape(bs, ra, aw)))
        oc = _band_attention(per_seq(cq), per_seq(ck), per_seq(cv), t5_s, sinks, rc, ss,
                             prev=(cache_c_k[l].reshape(bs, rc, kvw), cache_c_v[l].reshape(bs, rc, kvw)))
        st_s.append((ak.reshape(bs, ss, a_heads, HEAD_DIM), av.reshape(bs, ss, a_heads, HEAD_DIM),
                     ck.reshape(bs, ss, c_kv_heads, HEAD_DIM), cv.reshape(bs, ss, c_kv_heads, HEAD_DIM),
                     bv.reshape(bs, ss, aw)))
        xs_mid, h2_s, ids_s, gates_s, rank_s, counts = _merge(
            oa.reshape(1, ts, aw), oc.reshape(1, ts, aw), bu, bv, g, xs, ms[2], ms[3], ms[4], lw,
            sgu_w_s, sgu_b_s, counts_p, ts)

        ids = jnp.concatenate([_tiles_to_tokens(ids_p), _tiles_to_tokens(ids_s)], axis=1)
        ranks = jnp.concatenate([_tiles_to_tokens(rank_p), _tiles_to_tokens(rank_s)], axis=1)
        pos, block_e, n_used = _route(ids, ranks, counts[:, 0].astype(jnp.int32), MOE_ROWS, n_blocks)
        pos_tk = jnp.transpose(pos)
        pos_p = pos_tk[:tp].reshape(-1)
        pos_s = pos_tk[tp:].reshape(-1)
        rows_in = jnp.zeros((n_rows, d), F32)
        rows_in = _dispatch(pos_p, h2_p.reshape(tp, d), rows_in, moe_tile)
        rows_in = _dispatch(pos_s, h2_s.reshape(ts, d), rows_in, ts)
        rows_out = _experts(block_e, n_used, rows_in, w_gate_up[l], b_gate_up[l], w_down[l],
                            b_down[l], MOE_ROWS)
        gates_p_tk = jnp.transpose(_tiles_to_tokens(gates_p)).reshape(bp, sp, TOP_K)
        gates_s_tk = jnp.transpose(_tiles_to_tokens(gates_s)).reshape(1, ts, TOP_K)
        xp = _combine(pos_p, rows_out, gates_p_tk, xp_mid, mp[5], moe_tile)
        xs = _combine(pos_s, rows_out, gates_s_tk, xs_mid, ms[5], ts)

    stack = lambda rows, i: jnp.stack([r[i] for r in rows], axis=0)
    return (xp, xs.reshape(bs, ss, d),
            stack(st_p, 0), stack(st_p, 1), stack(st_p, 2), stack(st_p, 3),
            stack(st_s, 0), stack(st_s, 1), stack(st_s, 2), stack(st_s, 3), stack(st_s, 4))
```

```python
import functools
import math

import jax
import jax.numpy as jnp
import numpy as np
from jax import lax
from jax.experimental import pallas as pl
from jax.experimental.pallas import tpu as pltpu

F32 = jnp.float32
BF16 = jnp.bfloat16

CHUNK = 64
HEAD_DIM = 64
A_PREV_CHUNKS = 8
A_REL_CLIP = 128
B_CHUNK = 128
B_GROUPS = 4
C_PREV_CHUNKS = 2
T5_BUCKETS = 32
T5_MAX_DIST = 128
N_BRANCH = 3
N_EXPERTS = 32
TOP_K = 4
SWIGLU_ALPHA = 1.702
SWIGLU_LIMIT = 7.0
RMS_EPS = 1e-6
LN_EPS = 1e-5
NEG_INF = -1e30
F32_LOWEST = float(np.finfo(np.float32).min)

VMEM_LIMIT_BYTES = 56 * 1024 * 1024
ADA_ROWS = 16
PROJ_TILE = 256
A_Q_TILE = 256
C_Q_TILE = 128
MOE_ROWS = 512
MOE_TOKEN_TILE = 256


def _cparams(n_axes, vmem=None):
    return pltpu.CompilerParams(dimension_semantics=("arbitrary",) * n_axes,
                                vmem_limit_bytes=vmem)


def _const_spec(shape):
    nd = len(shape)
    return pl.BlockSpec(shape, lambda *_: (0,) * nd)


def _ada_kernel(c_ref, w_ref, b_ref, o_ref):
    c = c_ref[...]
    s = c * jax.nn.sigmoid(c)
    o_ref[0] = jnp.dot(s, w_ref[0], preferred_element_type=F32,
                       precision=lax.Precision.HIGHEST) + b_ref[0]


def _ada_modulation(c_rows, w_ada, b_ada):
    depth, d, _ = w_ada.shape
    return pl.pallas_call(
        _ada_kernel,
        grid=(depth, 6),
        in_specs=[pl.BlockSpec((ADA_ROWS, d), lambda l, j: (0, 0)),
                  pl.BlockSpec((1, d, d), lambda l, j: (l, 0, j)),
                  pl.BlockSpec((1, 1, d), lambda l, j: (l, 0, j))],
        out_specs=pl.BlockSpec((1, ADA_ROWS, d), lambda l, j: (l, 0, j)),
        out_shape=jax.ShapeDtypeStruct((depth, ADA_ROWS, 6 * d), F32),
        compiler_params=_cparams(2),
    )(c_rows, w_ada, b_ada.reshape(depth, 1, 6 * d))


def _row_rms(x, gain):
    return x * lax.rsqrt(jnp.mean(x * x, axis=-1, keepdims=True) + RMS_EPS) * gain


def _head_rms(t, gain, bd):
    sq = t * t
    hi = sq.astype(BF16)
    lo = (sq - hi.astype(F32)).astype(BF16)
    ss = (jnp.dot(hi, bd, preferred_element_type=F32)
          + jnp.dot(lo, bd, preferred_element_type=F32))
    return t * lax.rsqrt(ss * (1.0 / HEAD_DIM) + RMS_EPS) * gain


def _proj_kernel(x_ref, shift_ref, scale_ref, gmix_ref,
                 w_aq, w_ak, w_av, w_bu, w_bv, w_cq, w_ck, w_cv, w_g,
                 bd_ref, aqn, akn, cqn, ckn, lng, lnb,
                 aq_o, ak_o, av_o, bu_o, bv_o, cq_o, ck_o, cv_o, g_o):
    x = x_ref[0]
    h = _row_rms(x, gmix_ref[...]) * (1.0 + scale_ref[0]) + shift_ref[0]
    hb = h.astype(BF16)

    def mm(w_ref):
        return jnp.dot(hb, w_ref[...], preferred_element_type=F32)

    bd = bd_ref[...]
    q_scale = HEAD_DIM ** -0.5
    aq_o[0] = (_head_rms(mm(w_aq), aqn[...], bd) * q_scale).astype(BF16)
    ak_o[0] = _head_rms(mm(w_ak), akn[...], bd)
    av_o[0] = mm(w_av)
    bu_o[0] = jax.nn.gelu(mm(w_bu)).astype(BF16)
    bv = jax.nn.gelu(mm(w_bv))
    mu = jnp.mean(bv, axis=-1, keepdims=True)
    var = jnp.mean(jnp.square(bv - mu), axis=-1, keepdims=True)
    bv_o[0] = (bv - mu) * lax.rsqrt(var + LN_EPS) * lng[...] + lnb[...]
    cq_o[0] = (_head_rms(mm(w_cq), cqn[...], bd) * q_scale).astype(BF16)
    kvw = ck_o.shape[-1]
    ck_o[0] = _head_rms(mm(w_ck), ckn[...], bd[:kvw, :kvw])
    cv_o[0] = mm(w_cv)
    d = x.shape[-1]
    for r in range(N_BRANCH):
        g_o[0, :, r * d:(r + 1) * d] = jax.nn.sigmoid(
            jnp.dot(hb, w_g[:, r * d:(r + 1) * d], preferred_element_type=F32)).astype(BF16)


def _project(x, shift, scale, lw, tm):
    b, s, d = x.shape
    r = shift.shape[1]
    widths = [w.shape[1] for w in lw["w_split"]]
    aw, kvw = widths[0], widths[6]
    tok = lambda w: pl.BlockSpec((1, tm, w), lambda bi, i: (bi, i, 0))
    mod = pl.BlockSpec((1, r, d), lambda bi, i: (bi, 0, 0))
    in_specs = ([tok(d), mod, mod, _const_spec((1, d))]
                + [_const_spec(w.shape) for w in lw["w_split"]]
                + [_const_spec(lw["bd"].shape)]
                + [_const_spec((1, aw)), _const_spec((1, aw)), _const_spec((1, aw)),
                   _const_spec((1, kvw)), _const_spec((1, aw)), _const_spec((1, aw))])
    out_dt = [BF16, F32, F32, BF16, F32, BF16, F32, F32, BF16]
    return pl.pallas_call(
        _proj_kernel,
        grid=(b, s // tm),
        in_specs=in_specs,
        out_specs=[tok(w) for w in widths],
        out_shape=[jax.ShapeDtypeStruct((b, s, w), dt) for w, dt in zip(widths, out_dt)],
        compiler_params=_cparams(2, VMEM_LIMIT_BYTES),
    )(x, shift, scale, lw["g_mix"], *lw["w_split"], lw["bd"],
      lw["aqn"], lw["akn"], lw["cqn"], lw["ckn"], lw["lng"], lw["lnb"])


def _attn_kernel(*refs, n_q_heads, n_kv_heads, has_sink, carry):
    refs = list(refs)
    q_ref, k_ref, v_ref = refs[:3]
    pos = 3
    if not carry:
        pk_ref, pv_ref = refs[pos:pos + 2]
        pos += 2
    bias_ref = refs[pos]
    pos += 1
    if has_sink:
        sink_ref = refs[pos]
        pos += 1
    o_ref = refs[pos]
    pos += 1
    if carry:
        pk_scr, pv_scr = refs[pos:pos + 2]

    j = pl.program_id(1)
    kc = k_ref[0].astype(BF16)
    vc = v_ref[0].astype(BF16)
    tq = kc.shape[0]
    if carry:
        @pl.when(j == 0)
        def _():
            pk_scr[...] = jnp.zeros_like(pk_scr)
            pv_scr[...] = jnp.zeros_like(pv_scr)
        kp, vp = pk_scr[...], pv_scr[...]
    else:
        kp, vp = pk_ref[0].astype(BF16), pv_ref[0].astype(BF16)
    n_prev = kp.shape[0]
    kw = jnp.concatenate([kp, kc], axis=0)
    vw = jnp.concatenate([vp, vc], axis=0)
    width = n_prev + tq
    if carry:
        before_start = lax.broadcasted_iota(jnp.int32, (tq, width), 1) < (n_prev - j * tq)
    q = q_ref[0]
    group = n_q_heads // n_kv_heads
    outs = []
    for h in range(n_q_heads):
        kh = h // group
        qh = q[:, h * HEAD_DIM:(h + 1) * HEAD_DIM]
        s = lax.dot_general(qh, kw[:, kh * HEAD_DIM:(kh + 1) * HEAD_DIM],
                            (((1,), (1,)), ((), ())), preferred_element_type=F32) + bias_ref[h]
        if carry:
            s = jnp.where(before_start, NEG_INF, s)
        m = jnp.max(s, axis=-1, keepdims=True)
        if has_sink:
            sink = sink_ref[h]
            m = jnp.maximum(m, sink)
        e = jnp.exp(s - m)
        denom = jnp.sum(e, axis=-1, keepdims=True)
        if has_sink:
            denom = denom + jnp.exp(sink - m)
        o = jnp.dot(e.astype(BF16), vw[:, kh * HEAD_DIM:(kh + 1) * HEAD_DIM],
                    preferred_element_type=F32)
        outs.append(o / denom)
    o_ref[0] = jnp.concatenate(outs, axis=-1).astype(BF16)
    if carry:
        pk_scr[...] = kw[tq:, :]
        pv_scr[...] = vw[tq:, :]


def _band_attention(q, k, v, bias, sinks, n_prev_rows, tq, prev=None):
    b, s, qw = q.shape
    kvw = k.shape[-1]
    hq, kvh = qw // HEAD_DIM, kvw // HEAD_DIM
    carry = prev is None
    width = n_prev_rows + tq
    tok = lambda w: pl.BlockSpec((1, tq, w), lambda bi, i: (bi, i, 0))
    in_specs = [tok(qw), tok(kvw), tok(kvw)]
    args = [q, k, v]
    if not carry:
        pspec = pl.BlockSpec((1, n_prev_rows, kvw), lambda bi, i: (bi, 0, 0))
        in_specs += [pspec, pspec]
        args += list(prev)
    in_specs.append(_const_spec((hq, tq, width)))
    args.append(bias)
    if sinks is not None:
        in_specs.append(pl.BlockSpec(memory_space=pltpu.SMEM))
        args.append(sinks)
    scratch = [pltpu.VMEM((n_prev_rows, kvw), BF16)] * 2 if carry else []
    return pl.pallas_call(
        functools.partial(_attn_kernel, n_q_heads=hq, n_kv_heads=kvh,
                          has_sink=sinks is not None, carry=carry),
        grid=(b, s // tq),
        in_specs=in_specs,
        out_specs=tok(qw),
        out_shape=jax.ShapeDtypeStruct((b, s, qw), BF16),
        scratch_shapes=scratch,
        compiler_params=_cparams(2, VMEM_LIMIT_BYTES),
    )(*args)


def _band_bias(base, tq, n_prev_chunks):
    n_qc = tq // CHUNK
    blocks = [jnp.pad(base, ((0, 0), (0, 0), (jq * CHUNK, (n_qc - 1 - jq) * CHUNK)),
                      constant_values=NEG_INF) for jq in range(n_qc)]
    return jnp.concatenate(blocks, axis=1)


def _toeplitz(vec, n_q, n_k):
    return jnp.stack([vec[:, n_q - 1 - i:n_q - 1 - i + n_k] for i in range(n_q)], axis=1)


def _a_bias_table(a_rel, q0, n_q, n_k):
    t = np.arange(n_q + n_k - 1)
    idx = np.clip(q0 + n_q - 1 - t, -A_REL_CLIP, A_REL_CLIP) + A_REL_CLIP
    return _toeplitz(a_rel[:, idx].astype(F32), n_q, n_k)


def _t5_bucket(rel):
    n = T5_BUCKETS // 2
    max_exact = n // 2
    ret = jnp.where(rel > 0, n, 0)
    a = jnp.abs(rel)
    large = max_exact + (jnp.log(jnp.maximum(a, 1).astype(F32) / max_exact)
                         / math.log(T5_MAX_DIST / max_exact) * (n - max_exact)).astype(jnp.int32)
    large = jnp.minimum(large, n - 1)
    return ret + jnp.where(a < max_exact, a, large)


def _c_bias_table(t5_table, q0, n_q, n_k):
    rel = jnp.arange(n_q + n_k - 1, dtype=jnp.int32) - (n_q - 1) - q0
    vec = jnp.transpose(t5_table[_t5_bucket(rel)]).astype(F32)
    return _toeplitz(vec, n_q, n_k)


def _merge_kernel(oa_ref, oc_ref, bu_ref, bv_ref, g_ref, x_ref, gate1_ref, shift2_ref, scale2_ref,
                  gffn_ref, wbr_ref, wout_ref, ws_ref, bsb_ref, wrt_ref, br_ref, tri_ref, cnt_ref,
                  xnew_o, h2_o, ids_o, gates_o, rank_o, cnt_o, base_scr):
    first = jnp.logical_and(pl.program_id(0) == 0, pl.program_id(1) == 0)

    @pl.when(first)
    def _():
        base_scr[...] = cnt_ref[...]

    tm, d = x_ref.shape[1], x_ref.shape[2]
    bw = bv_ref.shape[2]
    gch = bw // B_GROUPS
    bvb = bv_ref[0].astype(BF16)
    rows = []
    for r in range(tm // B_CHUNK):
        cols = [jnp.dot(ws_ref[gi], bvb[r * B_CHUNK:(r + 1) * B_CHUNK, gi * gch:(gi + 1) * gch],
                        preferred_element_type=F32) for gi in range(B_GROUPS)]
        rows.append(jnp.concatenate(cols, axis=-1) + bsb_ref[...])
    mixed = jnp.concatenate(rows, axis=0) if len(rows) > 1 else rows[0]
    ob = (bu_ref[0].astype(F32) * mixed).astype(BF16)

    branches = (oa_ref[0], ob, oc_ref[0])
    y = None
    for r in range(N_BRANCH):
        br = jnp.dot(branches[r], wbr_ref[r], preferred_element_type=F32)
        term = g_ref[0, :, r * d:(r + 1) * d].astype(F32) * br
        y = term if y is None else y + term
    out = jnp.dot(y.astype(BF16), wout_ref[...], preferred_element_type=F32)
    xn = x_ref[0] + gate1_ref[0] * out
    xnew_o[0] = xn
    h2 = _row_rms(xn, gffn_ref[...]) * (1.0 + scale2_ref[0]) + shift2_ref[0]
    h2_o[0] = h2

    nt = (((1,), (1,)), ((), ()))
    h_hi = h2.astype(BF16)
    h_lo = (h2 - h_hi.astype(F32)).astype(BF16)
    logits = (lax.dot_general(wrt_ref[0], h_hi, nt, preferred_element_type=F32)
              + lax.dot_general(wrt_ref[0], h_lo, nt, preferred_element_type=F32)
              + lax.dot_general(wrt_ref[1], h_hi, nt, preferred_element_type=F32)) + br_ref[...]
    n_exp = logits.shape[0]
    eidx = lax.broadcasted_iota(jnp.int32, (n_exp, tm), 0).astype(F32)
    vals, ids = [], []
    rest = logits
    for _ in range(TOP_K):
        m = jnp.max(rest, axis=0, keepdims=True)
        sel = jnp.min(jnp.where(rest == m, eidx, float(n_exp)), axis=0, keepdims=True)
        vals.append(m)
        ids.append(sel)
        rest = jnp.where(eidx == sel, F32_LOWEST, rest)
    exps = [jnp.exp(v - vals[0]) for v in vals]
    tot = exps[0] + exps[1] + exps[2] + exps[3]
    gates_o[0] = jnp.concatenate([e / tot for e in exps], axis=0)
    ids_o[0] = jnp.concatenate(ids, axis=0).astype(jnp.int32)

    base = base_scr[...]
    ranks = []
    for k in range(TOP_K):
        hit = eidx == ids[k]
        onehot = jnp.where(hit, 1.0, 0.0)
        before = jnp.dot(onehot.astype(BF16), tri_ref[...], preferred_element_type=F32)
        ranks.append(jnp.sum(jnp.where(hit, base[:, 0:1] + before, 0.0), axis=0, keepdims=True))
        base = base + jnp.sum(onehot, axis=1, keepdims=True)
    rank_o[0] = jnp.concatenate(ranks, axis=0).astype(jnp.int32)
    base_scr[...] = base
    cnt_o[...] = base


def _merge(oa, oc, bu, bv, g, x, gate1, shift2, scale2, lw, sgu_w, sgu_b, counts, tm):
    b, s, d = x.shape
    r = gate1.shape[1]
    bw = bu.shape[-1]
    n_tiles = b * (s // tm)
    tok = lambda w: pl.BlockSpec((1, tm, w), lambda bi, i: (bi, i, 0))
    mod = pl.BlockSpec((1, r, d), lambda bi, i: (bi, 0, 0))
    per_tile = pl.BlockSpec((1, TOP_K, tm), lambda bi, i: (bi * (s // tm) + i, 0, 0))
    tri = jnp.asarray(np.triu(np.ones((tm, tm), np.float32), k=1), BF16)
    in_specs = [tok(bw), tok(bw), tok(bw), tok(bw), tok(N_BRANCH * d), tok(d), mod, mod, mod,
                _const_spec((1, d)), _const_spec(lw["w_br"].shape), _const_spec((d, d)),
                _const_spec(sgu_w.shape), _const_spec(sgu_b.shape),
                _const_spec((2, N_EXPERTS, d)), _const_spec((N_EXPERTS, 1)),
                _const_spec((tm, tm)), _const_spec((N_EXPERTS, 128))]
    out_shape = [jax.ShapeDtypeStruct((b, s, d), F32), jax.ShapeDtypeStruct((b, s, d), F32),
                 jax.ShapeDtypeStruct((n_tiles, TOP_K, tm), jnp.int32),
                 jax.ShapeDtypeStruct((n_tiles, TOP_K, tm), F32),
                 jax.ShapeDtypeStruct((n_tiles, TOP_K, tm), jnp.int32),
                 jax.ShapeDtypeStruct((N_EXPERTS, 128), F32)]
    out_specs = [tok(d), tok(d), per_tile, per_tile, per_tile, _const_spec((N_EXPERTS, 128))]
    return pl.pallas_call(
        _merge_kernel,
        grid=(b, s // tm),
        in_specs=in_specs,
        out_specs=out_specs,
        out_shape=out_shape,
        scratch_shapes=[pltpu.VMEM((N_EXPERTS, 128), F32)],
        compiler_params=_cparams(2, VMEM_LIMIT_BYTES),
    )(oa, oc, bu, bv, g, x, gate1, shift2, scale2, lw["g_ffn"], lw["w_br"], lw["w_out"],
      sgu_w, sgu_b, lw["w_rt"], lw["b_r"], tri, counts)


def _row_copy(src_ref, src_row, dst_ref, dst_row, sem):
    return pltpu.make_async_copy(src_ref.at[pl.ds(src_row, 1)], dst_ref.at[pl.ds(dst_row, 1)], sem)


def _dispatch_kernel(pos_ref, h_ref, xs_in_ref, xs_ref, sem):
    del xs_in_ref
    tm = h_ref.shape[0]
    base = pl.program_id(0) * (tm * TOP_K)

    def issue(t, carry):
        for k in range(TOP_K):
            _row_copy(h_ref, t, xs_ref, pos_ref[base + t * TOP_K + k], sem).start()
        return carry

    lax.fori_loop(0, tm, issue, 0)

    def drain(t, carry):
        for k in range(TOP_K):
            _row_copy(h_ref, 0, xs_ref, 0, sem).wait()
        return carry

    lax.fori_loop(0, tm, drain, 0)


def _dispatch(pos_flat, h, xs, tm):
    t, d = h.shape
    return pl.pallas_call(
        _dispatch_kernel,
        grid_spec=pltpu.PrefetchScalarGridSpec(
            num_scalar_prefetch=1,
            grid=(t // tm,),
            in_specs=[pl.BlockSpec((tm, d), lambda i, pos: (i, 0)),
                      pl.BlockSpec(memory_space=pl.ANY)],
            out_specs=pl.BlockSpec(memory_space=pl.ANY),
            scratch_shapes=[pltpu.SemaphoreType.DMA]),
        out_shape=jax.ShapeDtypeStruct(xs.shape, xs.dtype),
        input_output_aliases={2: 0},
        compiler_params=_cparams(1),
    )(pos_flat, h, xs)


def _expert_kernel(be_ref, nused_ref, xs_ref, wgu_ref, bgu_ref, wdn_ref, bdn_ref, ys_ref,
                   wgu_scr, wdn_scr):
    i = pl.program_id(0)
    prev_e = be_ref[jnp.maximum(i - 1, 0)]
    new_expert = jnp.logical_or(i == 0, be_ref[i] != prev_e)
    active = i < nused_ref[0]

    @pl.when(jnp.logical_and(active, new_expert))
    def _():
        wgu_scr[...] = wgu_ref[0].astype(BF16)
        wdn_scr[...] = wdn_ref[0].astype(BF16)

    @pl.when(active)
    def _():
        de = wdn_scr.shape[0]
        gu = jnp.dot(xs_ref[...].astype(BF16), wgu_scr[...],
                     preferred_element_type=F32) + bgu_ref[0]
        g = jnp.minimum(gu[:, :de], SWIGLU_LIMIT)
        u = jnp.clip(gu[:, de:], -SWIGLU_LIMIT, SWIGLU_LIMIT)
        act = (u + 1.0) * g * jax.nn.sigmoid(SWIGLU_ALPHA * g)
        ys_ref[...] = jnp.dot(act.astype(BF16), wdn_scr[...],
                              preferred_element_type=F32) + bdn_ref[0]

    @pl.when(jnp.logical_not(active))
    def _():
        ys_ref[...] = jnp.zeros_like(ys_ref)


def _experts(block_e, n_used, xs, w_gu, b_gu, w_dn, b_dn, rows):
    n_rows, d = xs.shape
    n_e, _, two_de = w_gu.shape
    de = two_de // 2
    n_blocks = n_rows // rows
    row_map = lambda i, be, nu: (jnp.minimum(i, nu[0] - 1), 0)
    e_map = lambda i, be, nu: (be[jnp.minimum(i, nu[0] - 1)], 0, 0)
    return pl.pallas_call(
        _expert_kernel,
        grid_spec=pltpu.PrefetchScalarGridSpec(
            num_scalar_prefetch=2,
            grid=(n_blocks,),
            in_specs=[pl.BlockSpec((rows, d), row_map),
                      pl.BlockSpec((1, d, two_de), e_map),
                      pl.BlockSpec((1, 1, two_de), e_map),
                      pl.BlockSpec((1, de, d), e_map),
                      pl.BlockSpec((1, 1, d), e_map)],
            out_specs=pl.BlockSpec((rows, d), lambda i, be, nu: (i, 0)),
            scratch_shapes=[pltpu.VMEM((d, two_de), BF16), pltpu.VMEM((de, d), BF16)]),
        out_shape=jax.ShapeDtypeStruct((n_rows, d), F32),
        compiler_params=_cparams(1, VMEM_LIMIT_BYTES),
    )(block_e, n_used, xs, w_gu, b_gu.reshape(n_e, 1, two_de), w_dn, b_dn.reshape(n_e, 1, d))


def _combine_kernel(pos_ref, ys_ref, gates_ref, x_ref, gate2_ref, o_ref, buf, sem):
    tm = x_ref.shape[1]
    n_i = pl.num_programs(1)
    base = (pl.program_id(0) * n_i + pl.program_id(1)) * (tm * TOP_K)

    def issue(t, carry):
        for k in range(TOP_K):
            _row_copy(ys_ref, pos_ref[base + t * TOP_K + k], buf.at[k], t, sem).start()
        return carry

    lax.fori_loop(0, tm, issue, 0)

    def drain(t, carry):
        for k in range(TOP_K):
            _row_copy(ys_ref, 0, buf.at[k], 0, sem).wait()
        return carry

    lax.fori_loop(0, tm, drain, 0)

    gates = gates_ref[0]
    ff = gates[:, 0:1] * buf[0]
    for k in range(1, TOP_K):
        ff = ff + gates[:, k:k + 1] * buf[k]
    o_ref[0] = x_ref[0] + gate2_ref[0] * ff


def _combine(pos_flat, ys, gates_tk, x, gate2, tm):
    b, s, d = x.shape
    r = gate2.shape[1]
    return pl.pallas_call(
        _combine_kernel,
        grid_spec=pltpu.PrefetchScalarGridSpec(
            num_scalar_prefetch=1,
            grid=(b, s // tm),
            in_specs=[pl.BlockSpec(memory_space=pl.ANY),
                      pl.BlockSpec((1, tm, TOP_K), lambda bi, i, pos: (bi, i, 0)),
                      pl.BlockSpec((1, tm, d), lambda bi, i, pos: (bi, i, 0)),
                      pl.BlockSpec((1, r, d), lambda bi, i, pos: (bi, 0, 0))],
            out_specs=pl.BlockSpec((1, tm, d), lambda bi, i, pos: (bi, i, 0)),
            scratch_shapes=[pltpu.VMEM((TOP_K, tm, d), F32), pltpu.SemaphoreType.DMA]),
        out_shape=jax.ShapeDtypeStruct((b, s, d), F32),
        compiler_params=_cparams(2, VMEM_LIMIT_BYTES),
    )(pos_flat, ys, gates_tk, x, gate2)


def _route(ids, ranks, counts, rows, n_blocks):
    padded = (counts + rows - 1) // rows * rows
    pad_end = jnp.cumsum(padded)
    pad_start = pad_end - padded
    experts = jnp.arange(N_EXPERTS, dtype=jnp.int32)
    pos = jnp.sum(jnp.where(ids[..., None] == experts, pad_start, 0), axis=-1) + ranks
    first_row = jnp.arange(n_blocks, dtype=jnp.int32) * rows
    block_e = jnp.minimum(jnp.sum((pad_end[None, :] <= first_row[:, None]).astype(jnp.int32), axis=1),
                          N_EXPERTS - 1)
    n_used = jnp.maximum(pad_end[-1:] // rows, 1).astype(jnp.int32)
    return pos.astype(jnp.int32), block_e, n_used


def _tiles_to_tokens(a):
    n_tiles, k, tm = a.shape
    return jnp.transpose(a, (1, 0, 2)).reshape(k, n_tiles * tm)


def _split_hi_lo(w):
    hi = w.astype(BF16)
    return jnp.stack([hi, (w - hi.astype(F32)).astype(BF16)])


def _layer_weights(l, w_in, g_norm_mix, g_norm_ffn, a_q_norm, a_k_norm, b_ln_g, b_ln_b,
                   c_q_norm, c_k_norm, w_branch, w_out, w_router, b_router, widths):
    d = w_in.shape[1]
    aw, kvw = widths[0], widths[6]
    splits = np.cumsum(widths)[:-1].tolist()
    w_split = [w.astype(BF16) for w in jnp.split(w_in[l], splits, axis=-1)]
    head = np.arange(aw) // HEAD_DIM
    bd = jnp.asarray(head[:, None] == head[None, :], BF16)
    tile = lambda v, w: jnp.tile(v, w // HEAD_DIM).reshape(1, w)
    return dict(
        w_split=w_split, bd=bd,
        g_mix=g_norm_mix[l].reshape(1, d), g_ffn=g_norm_ffn[l].reshape(1, d),
        aqn=tile(a_q_norm[l], aw), akn=tile(a_k_norm[l], aw),
        cqn=tile(c_q_norm[l], aw), ckn=tile(c_k_norm[l], kvw),
        lng=b_ln_g[l].reshape(1, aw), lnb=b_ln_b[l].reshape(1, aw),
        w_br=w_branch[l].astype(BF16), w_out=w_out[l].astype(BF16),
        w_rt=_split_hi_lo(jnp.transpose(w_router[l])), b_r=b_router[l].reshape(N_EXPERTS, 1))


def kernel(x_prompt, x_sample, cache_a_k, cache_a_v, cache_c_k, cache_c_v, c_prompt, c_sample,
           w_ada, b_ada, g_norm_mix, g_norm_ffn, w_in, a_q_norm, a_k_norm, a_rel_bias,
           b_ln_g, b_ln_b, b_w_s, b_b_s, c_q_norm, c_k_norm, c_sinks, t5_bias,
           w_branch, w_out, w_router, b_router, w_gate_up, b_gate_up, w_down, b_down):
    bp, sp, d = x_prompt.shape
    bs, ss, _ = x_sample.shape
    depth = w_in.shape[0]
    ts = bs * ss
    tp = bp * sp
    a_heads, c_kv_heads = cache_a_k.shape[3], cache_c_k.shape[3]
    aw = a_heads * HEAD_DIM
    kvw = c_kv_heads * HEAD_DIM
    widths = [aw, aw, aw, aw, aw, aw, kvw, kvw, N_BRANCH * d]
    ra, rc = cache_a_k.shape[2], cache_c_k.shape[2]
    a_band, c_band = A_PREV_CHUNKS * CHUNK, C_PREV_CHUNKS * CHUNK
    proj_tile = min(PROJ_TILE, sp)
    a_tile, c_tile = min(A_Q_TILE, sp), min(C_Q_TILE, sp)
    moe_tile = min(MOE_TOKEN_TILE, tp)

    c_rows = jnp.concatenate([c_prompt, c_sample,
                              jnp.zeros((ADA_ROWS - bp - bs, d), F32)], axis=0)
    mod_all = _ada_modulation(c_rows, w_ada, b_ada).reshape(depth, ADA_ROWS, 6, d)

    t5_p = _c_bias_table(t5_bias, c_band, CHUNK, c_band + CHUNK)
    t5_s = _c_bias_table(t5_bias, rc, ss, rc + ss)
    bias_c_p = _band_bias(t5_p, c_tile, C_PREV_CHUNKS)

    n_assign = (tp + ts) * TOP_K
    n_blocks = -(-(n_assign + N_EXPERTS * (MOE_ROWS - 1)) // MOE_ROWS)
    n_rows = n_blocks * MOE_ROWS

    mask = (np.arange(B_CHUNK)[None, :] // CHUNK <= np.arange(B_CHUNK)[:, None] // CHUNK)
    mask = jnp.asarray(mask, F32)

    xp = x_prompt
    xs = x_sample.reshape(1, ts, d)
    st_p, st_s = [], []
    for l in range(depth):
        lw = _layer_weights(l, w_in, g_norm_mix, g_norm_ffn, a_q_norm, a_k_norm, b_ln_g, b_ln_b,
                            c_q_norm, c_k_norm, w_branch, w_out, w_router, b_router, widths)
        mod_p = mod_all[l, :bp]
        mod_s = jnp.repeat(mod_all[l, bp:bp + bs], ss, axis=0)
        mp = [mod_p[:, i].reshape(bp, 1, d) for i in range(6)]
        ms = [mod_s[:, i].reshape(1, ts, d) for i in range(6)]

        ws_m = b_w_s[l] * mask
        sgu_w_p = ws_m.astype(BF16)
        sgu_b_p = jnp.repeat(jnp.transpose(b_b_s[l]), aw // B_GROUPS, axis=1)
        eye = jnp.eye(B_CHUNK // ss, dtype=F32)
        sgu_w_s = jnp.stack([jnp.kron(eye, ws_m[g, :ss, :ss]) for g in range(B_GROUPS)]).astype(BF16)
        sgu_b_s = jnp.tile(sgu_b_p[:ss], (B_CHUNK // ss, 1))

        bias_a_p = _band_bias(_a_bias_table(a_rel_bias[l], a_band, CHUNK, a_band + CHUNK),
                              a_tile, A_PREV_CHUNKS)
        bias_a_s = _a_bias_table(a_rel_bias[l], ra, ss, ra + ss)
        sinks = c_sinks[l].astype(F32)

        aq, ak, av, bu, bv, cq, ck, cv, g = _project(xp, mp[0], mp[1], lw, proj_tile)
        oa = _band_attention(aq, ak, av, bias_a_p, None, a_band, a_tile)
        oc = _band_attention(cq, ck, cv, bias_c_p, sinks, c_band, c_tile)
        na, nc = min(a_band, sp), min(c_band, sp)
        st_p.append((ak[:, sp - na:].reshape(bp, na, a_heads, HEAD_DIM),
                     av[:, sp - na:].reshape(bp, na, a_heads, HEAD_DIM),
                     ck[:, sp - nc:].reshape(bp, nc, c_kv_heads, HEAD_DIM),
                     cv[:, sp - nc:].reshape(bp, nc, c_kv_heads, HEAD_DIM)))
        zero_counts = jnp.zeros((N_EXPERTS, 128), F32)
        xp_mid, h2_p, ids_p, gates_p, rank_p, counts_p = _merge(
            oa, oc, bu, bv, g, xp, mp[2], mp[3], mp[4], lw, sgu_w_p, sgu_b_p, zero_counts,
            proj_tile)

        aq, ak, av, bu, bv, cq, ck, cv, g = _project(xs, ms[0], ms[1], lw, ts)
        per_seq = lambda a: a.reshape(bs, ss, a.shape[-1])
        oa = _band_attention(per_seq(aq), per_seq(ak), per_seq(av), bias_a_s, None, ra, ss,
                             prev=(cache_a_k[l].reshape(bs, ra, aw), cache_a_v[l].reshape(bs, ra, aw)))
        oc = _band_attention(per_seq(cq), per_seq(ck), per_seq(cv), t5_s, sinks, rc, ss,
                             prev=(cache_c_k[l].reshape(bs, rc, kvw), cache_c_v[l].reshape(bs, rc, kvw)))
        st_s.append((ak.reshape(bs, ss, a_heads, HEAD_DIM), av.reshape(bs, ss, a_heads, HEAD_DIM),
                     ck.reshape(bs, ss, c_kv_heads, HEAD_DIM), cv.reshape(bs, ss, c_kv_heads, HEAD_DIM),
                     bv.reshape(bs, ss, aw)))
        xs_mid, h2_s, ids_s, gates_s, rank_s, counts = _merge(
            oa.reshape(1, ts, aw), oc.reshape(1, ts, aw), bu, bv, g, xs, ms[2], ms[3], ms[4], lw,
            sgu_w_s, sgu_b_s, counts_p, ts)

        ids = jnp.concatenate([_tiles_to_tokens(ids_p), _tiles_to_tokens(ids_s)], axis=1)
        ranks = jnp.concatenate([_tiles_to_tokens(rank_p), _tiles_to_tokens(rank_s)], axis=1)
        pos, block_e, n_used = _route(ids, ranks, counts[:, 0].astype(jnp.int32), MOE_ROWS, n_blocks)
        pos_tk = jnp.transpose(pos)
        pos_p = pos_tk[:tp].reshape(-1)
        pos_s = pos_tk[tp:].reshape(-1)
        rows_in = jnp.zeros((n_rows, d), F32)
        rows_in = _dispatch(pos_p, h2_p.reshape(tp, d), rows_in, moe_tile)
        rows_in = _dispatch(pos_s, h2_s.reshape(ts, d), rows_in, ts)
        n_e = w_gate_up.shape[1]
        rows_out = _experts(block_e + l * n_e, n_used, rows_in,
                            w_gate_up.reshape((depth * n_e,) + w_gate_up.shape[2:]),
                            b_gate_up.reshape(depth * n_e, -1),
                            w_down.reshape((depth * n_e,) + w_down.shape[2:]),
                            b_down.reshape(depth * n_e, -1), MOE_ROWS)
        gates_p_tk = jnp.transpose(_tiles_to_tokens(gates_p)).reshape(bp, sp, TOP_K)
        gates_s_tk = jnp.transpose(_tiles_to_tokens(gates_s)).reshape(1, ts, TOP_K)
        xp = _combine(pos_p, rows_out, gates_p_tk, xp_mid, mp[5], moe_tile)
        xs = _combine(pos_s, rows_out, gates_s_tk, xs_mid, ms[5], ts)

    stack = lambda rows, i: jnp.stack([r[i] for r in rows], axis=0)
    return (xp, xs.reshape(bs, ss, d),
            stack(st_p, 0), stack(st_p, 1), stack(st_p, 2), stack(st_p, 3),
            stack(st_s, 0), stack(st_s, 1), stack(st_s, 2), stack(st_s, 3), stack(st_s, 4))
```

```python
import functools
import math

import jax
import jax.numpy as jnp
import numpy as np
from jax import lax
from jax.experimental import pallas as pl
from jax.experimental.pallas import tpu as pltpu

F32 = jnp.float32
BF16 = jnp.bfloat16

CHUNK = 64
HEAD_DIM = 64
A_PREV_CHUNKS = 8
A_REL_CLIP = 128
B_CHUNK = 128
B_GROUPS = 4
C_PREV_CHUNKS = 2
T5_BUCKETS = 32
T5_MAX_DIST = 128
N_BRANCH = 3
N_EXPERTS = 32
TOP_K = 4
SWIGLU_ALPHA = 1.702
SWIGLU_LIMIT = 7.0
RMS_EPS = 1e-6
LN_EPS = 1e-5
NEG_INF = -1e30
F32_LOWEST = float(np.finfo(np.float32).min)

VMEM_LIMIT_BYTES = 56 * 1024 * 1024
ADA_ROWS = 16
PROJ_TILE = 512
MERGE_TILE = 256
A_Q_TILE = 256
C_Q_TILE = 128
MOE_ROWS = 512
MOE_TOKEN_TILE = 256


def _cparams(n_axes, vmem=None):
    return pltpu.CompilerParams(dimension_semantics=("arbitrary",) * n_axes,
                                vmem_limit_bytes=vmem)


def _const_spec(shape):
    nd = len(shape)
    return pl.BlockSpec(shape, lambda *_: (0,) * nd, pipeline_mode=pl.Buffered(1))


def _ada_kernel(c_ref, w_ref, b_ref, o_ref):
    c = c_ref[...]
    s = c * jax.nn.sigmoid(c)
    o_ref[0] = jnp.dot(s, w_ref[0], preferred_element_type=F32,
                       precision=lax.Precision.HIGHEST) + b_ref[0]


def _ada_modulation(c_rows, w_ada, b_ada):
    depth, d, _ = w_ada.shape
    return pl.pallas_call(
        _ada_kernel,
        grid=(depth, 6),
        in_specs=[pl.BlockSpec((ADA_ROWS, d), lambda l, j: (0, 0)),
                  pl.BlockSpec((1, d, d), lambda l, j: (l, 0, j)),
                  pl.BlockSpec((1, 1, d), lambda l, j: (l, 0, j))],
        out_specs=pl.BlockSpec((1, ADA_ROWS, d), lambda l, j: (l, 0, j)),
        out_shape=jax.ShapeDtypeStruct((depth, ADA_ROWS, 6 * d), F32),
        compiler_params=_cparams(2),
    )(c_rows, w_ada, b_ada.reshape(depth, 1, 6 * d))


def _row_rms(x, gain):
    return x * lax.rsqrt(jnp.mean(x * x, axis=-1, keepdims=True) + RMS_EPS) * gain


def _head_rms(t, gain, bd):
    ss = jnp.dot((t * t).astype(BF16), bd, preferred_element_type=F32)
    return t * lax.rsqrt(ss * (1.0 / HEAD_DIM) + RMS_EPS) * gain


def _proj_kernel(x_ref, shift_ref, scale_ref, gmix_ref,
                 w_aq, w_ak, w_av, w_bu, w_bv, w_cq, w_ck, w_cv, w_g,
                 bd_ref, aqn, akn, cqn, ckn, lng, lnb,
                 aq_o, ak_o, av_o, bu_o, bv_o, cq_o, ck_o, cv_o, g_o):
    x = x_ref[0]
    h = _row_rms(x, gmix_ref[...]) * (1.0 + scale_ref[0]) + shift_ref[0]
    hb = h.astype(BF16)

    def mm(w_ref):
        return jnp.dot(hb, w_ref[...], preferred_element_type=F32)

    bd = bd_ref[...]
    q_scale = HEAD_DIM ** -0.5
    aq_o[0] = (_head_rms(mm(w_aq), aqn[...], bd) * q_scale).astype(BF16)
    ak_o[0] = _head_rms(mm(w_ak), akn[...], bd)
    av_o[0] = mm(w_av)
    bu_o[0] = jax.nn.gelu(mm(w_bu)).astype(BF16)
    bv = jax.nn.gelu(mm(w_bv))
    mu = jnp.mean(bv, axis=-1, keepdims=True)
    var = jnp.mean(jnp.square(bv - mu), axis=-1, keepdims=True)
    bv_o[0] = (bv - mu) * lax.rsqrt(var + LN_EPS) * lng[...] + lnb[...]
    cq_o[0] = (_head_rms(mm(w_cq), cqn[...], bd) * q_scale).astype(BF16)
    kvw = ck_o.shape[-1]
    ck_o[0] = _head_rms(mm(w_ck), ckn[...], bd[:kvw, :kvw])
    cv_o[0] = mm(w_cv)
    d = x.shape[-1]
    for r in range(N_BRANCH):
        g_o[0, :, r * d:(r + 1) * d] = jax.nn.sigmoid(
            jnp.dot(hb, w_g[:, r * d:(r + 1) * d], preferred_element_type=F32)).astype(BF16)


def _project(x, shift, scale, lw, tm):
    b, s, d = x.shape
    r = shift.shape[1]
    widths = [w.shape[1] for w in lw["w_split"]]
    aw, kvw = widths[0], widths[6]
    tok = lambda w: pl.BlockSpec((1, tm, w), lambda bi, i: (bi, i, 0))
    mod = pl.BlockSpec((1, r, d), lambda bi, i: (bi, 0, 0))
    in_specs = ([tok(d), mod, mod, _const_spec((1, d))]
                + [_const_spec(w.shape) for w in lw["w_split"]]
                + [_const_spec(lw["bd"].shape)]
                + [_const_spec((1, aw)), _const_spec((1, aw)), _const_spec((1, aw)),
                   _const_spec((1, kvw)), _const_spec((1, aw)), _const_spec((1, aw))])
    out_dt = [BF16, F32, F32, BF16, F32, BF16, F32, F32, BF16]
    return pl.pallas_call(
        _proj_kernel,
        grid=(b, s // tm),
        in_specs=in_specs,
        out_specs=[tok(w) for w in widths],
        out_shape=[jax.ShapeDtypeStruct((b, s, w), dt) for w, dt in zip(widths, out_dt)],
        compiler_params=_cparams(2, VMEM_LIMIT_BYTES),
    )(x, shift, scale, lw["g_mix"], *lw["w_split"], lw["bd"],
      lw["aqn"], lw["akn"], lw["cqn"], lw["ckn"], lw["lng"], lw["lnb"])


def _attn_kernel(*refs, n_q_heads, n_kv_heads, has_sink, carry):
    refs = list(refs)
    q_ref, k_ref, v_ref = refs[:3]
    pos = 3
    if not carry:
        pk_ref, pv_ref = refs[pos:pos + 2]
        pos += 2
    bias_ref = refs[pos]
    pos += 1
    if has_sink:
        sink_ref = refs[pos]
        pos += 1
    o_ref = refs[pos]
    pos += 1
    if carry:
        pk_scr, pv_scr = refs[pos:pos + 2]

    j = pl.program_id(1)
    kc = k_ref[0].astype(BF16)
    vc = v_ref[0].astype(BF16)
    tq = kc.shape[0]
    if carry:
        @pl.when(j == 0)
        def _():
            pk_scr[...] = jnp.zeros_like(pk_scr)
            pv_scr[...] = jnp.zeros_like(pv_scr)
        kp, vp = pk_scr[...], pv_scr[...]
    else:
        kp, vp = pk_ref[0].astype(BF16), pv_ref[0].astype(BF16)
    n_prev = kp.shape[0]
    kw = jnp.concatenate([kp, kc], axis=0)
    vw = jnp.concatenate([vp, vc], axis=0)
    width = n_prev + tq
    if carry:
        before_start = lax.broadcasted_iota(jnp.int32, (tq, width), 1) < (n_prev - j * tq)
    q = q_ref[0]
    group = n_q_heads // n_kv_heads
    outs = []
    for h in range(n_q_heads):
        kh = h // group
        qh = q[:, h * HEAD_DIM:(h + 1) * HEAD_DIM]
        s = lax.dot_general(qh, kw[:, kh * HEAD_DIM:(kh + 1) * HEAD_DIM],
                            (((1,), (1,)), ((), ())), preferred_element_type=F32) + bias_ref[h]
        if carry:
            s = jnp.where(before_start, NEG_INF, s)
        m = jnp.max(s, axis=-1, keepdims=True)
        if has_sink:
            sink = sink_ref[h]
            m = jnp.maximum(m, sink)
        e = jnp.exp(s - m)
        denom = jnp.sum(e, axis=-1, keepdims=True)
        if has_sink:
            denom = denom + jnp.exp(sink - m)
        o = jnp.dot(e.astype(BF16), vw[:, kh * HEAD_DIM:(kh + 1) * HEAD_DIM],
                    preferred_element_type=F32)
        outs.append(o / denom)
    o_ref[0] = jnp.concatenate(outs, axis=-1).astype(BF16)
    if carry:
        pk_scr[...] = kw[tq:, :]
        pv_scr[...] = vw[tq:, :]


def _band_attention(q, k, v, bias, sinks, n_prev_rows, tq, prev=None):
    b, s, qw = q.shape
    kvw = k.shape[-1]
    hq, kvh = qw // HEAD_DIM, kvw // HEAD_DIM
    carry = prev is None
    width = n_prev_rows + tq
    tok = lambda w: pl.BlockSpec((1, tq, w), lambda bi, i: (bi, i, 0))
    in_specs = [tok(qw), tok(kvw), tok(kvw)]
    args = [q, k, v]
    if not carry:
        pspec = pl.BlockSpec((1, n_prev_rows, kvw), lambda bi, i: (bi, 0, 0))
        in_specs += [pspec, pspec]
        args += list(prev)
    in_specs.append(_const_spec((hq, tq, width)))
    args.append(bias)
    if sinks is not None:
        in_specs.append(pl.BlockSpec(memory_space=pltpu.SMEM))
        args.append(sinks)
    scratch = [pltpu.VMEM((n_prev_rows, kvw), BF16)] * 2 if carry else []
    return pl.pallas_call(
        functools.partial(_attn_kernel, n_q_heads=hq, n_kv_heads=kvh,
                          has_sink=sinks is not None, carry=carry),
        grid=(b, s // tq),
        in_specs=in_specs,
        out_specs=tok(qw),
        out_shape=jax.ShapeDtypeStruct((b, s, qw), BF16),
        scratch_shapes=scratch,
        compiler_params=_cparams(2, VMEM_LIMIT_BYTES),
    )(*args)


def _band_bias(base, tq, n_prev_chunks):
    n_qc = tq // CHUNK
    blocks = [jnp.pad(base, ((0, 0), (0, 0), (jq * CHUNK, (n_qc - 1 - jq) * CHUNK)),
                      constant_values=NEG_INF) for jq in range(n_qc)]
    return jnp.concatenate(blocks, axis=1)


def _toeplitz(vec, n_q, n_k):
    return jnp.stack([vec[:, n_q - 1 - i:n_q - 1 - i + n_k] for i in range(n_q)], axis=1)


def _a_bias_table(a_rel, q0, n_q, n_k):
    t = np.arange(n_q + n_k - 1)
    idx = np.clip(q0 + n_q - 1 - t, -A_REL_CLIP, A_REL_CLIP) + A_REL_CLIP
    return _toeplitz(a_rel[:, idx].astype(F32), n_q, n_k)


def _t5_bucket(rel):
    n = T5_BUCKETS // 2
    max_exact = n // 2
    ret = jnp.where(rel > 0, n, 0)
    a = jnp.abs(rel)
    large = max_exact + (jnp.log(jnp.maximum(a, 1).astype(F32) / max_exact)
                         / math.log(T5_MAX_DIST / max_exact) * (n - max_exact)).astype(jnp.int32)
    large = jnp.minimum(large, n - 1)
    return ret + jnp.where(a < max_exact, a, large)


def _c_bias_table(t5_table, q0, n_q, n_k):
    rel = jnp.arange(n_q + n_k - 1, dtype=jnp.int32) - (n_q - 1) - q0
    vec = jnp.transpose(t5_table[_t5_bucket(rel)]).astype(F32)
    return _toeplitz(vec, n_q, n_k)


def _merge_kernel(oa_ref, oc_ref, bu_ref, bv_ref, g_ref, x_ref, gate1_ref, shift2_ref, scale2_ref,
                  gffn_ref, wbr_ref, wout_ref, ws_ref, bsb_ref, wrt_ref, br_ref, tri_ref, cnt_ref,
                  xnew_o, h2_o, ids_o, gates_o, rank_o, cnt_o, base_scr):
    first = jnp.logical_and(pl.program_id(0) == 0, pl.program_id(1) == 0)

    @pl.when(first)
    def _():
        base_scr[...] = cnt_ref[...]

    tm, d = x_ref.shape[1], x_ref.shape[2]
    bw = bv_ref.shape[2]
    gch = bw // B_GROUPS
    bvb = bv_ref[0].astype(BF16)
    rows = []
    for r in range(tm // B_CHUNK):
        cols = [jnp.dot(ws_ref[gi], bvb[r * B_CHUNK:(r + 1) * B_CHUNK, gi * gch:(gi + 1) * gch],
                        preferred_element_type=F32) for gi in range(B_GROUPS)]
        rows.append(jnp.concatenate(cols, axis=-1) + bsb_ref[...])
    mixed = jnp.concatenate(rows, axis=0) if len(rows) > 1 else rows[0]
    ob = (bu_ref[0].astype(F32) * mixed).astype(BF16)

    branches = (oa_ref[0], ob, oc_ref[0])
    y = None
    for r in range(N_BRANCH):
        br = jnp.dot(branches[r], wbr_ref[r], preferred_element_type=F32)
        term = g_ref[0, :, r * d:(r + 1) * d].astype(F32) * br
        y = term if y is None else y + term
    out = jnp.dot(y.astype(BF16), wout_ref[...], preferred_element_type=F32)
    xn = x_ref[0] + gate1_ref[0] * out
    xnew_o[0] = xn
    h2 = _row_rms(xn, gffn_ref[...]) * (1.0 + scale2_ref[0]) + shift2_ref[0]
    h2_o[0] = h2

    nt = (((1,), (1,)), ((), ()))
    h_hi = h2.astype(BF16)
    h_lo = (h2 - h_hi.astype(F32)).astype(BF16)
    logits = (lax.dot_general(wrt_ref[0], h_hi, nt, preferred_element_type=F32)
              + lax.dot_general(wrt_ref[0], h_lo, nt, preferred_element_type=F32)
              + lax.dot_general(wrt_ref[1], h_hi, nt, preferred_element_type=F32)) + br_ref[...]
    n_exp = logits.shape[0]
    eidx = lax.broadcasted_iota(jnp.int32, (n_exp, tm), 0).astype(F32)
    vals, ids = [], []
    rest = logits
    for _ in range(TOP_K):
        m = jnp.max(rest, axis=0, keepdims=True)
        sel = jnp.min(jnp.where(rest == m, eidx, float(n_exp)), axis=0, keepdims=True)
        vals.append(m)
        ids.append(sel)
        rest = jnp.where(eidx == sel, F32_LOWEST, rest)
    exps = [jnp.exp(v - vals[0]) for v in vals]
    tot = exps[0] + exps[1] + exps[2] + exps[3]
    gates_o[0] = jnp.concatenate([e / tot for e in exps], axis=0)
    ids_o[0] = jnp.concatenate(ids, axis=0).astype(jnp.int32)

    base = base_scr[...]
    ranks = []
    for k in range(TOP_K):
        hit = eidx == ids[k]
        onehot = jnp.where(hit, 1.0, 0.0)
        before = jnp.dot(onehot.astype(BF16), tri_ref[...], preferred_element_type=F32)
        ranks.append(jnp.sum(jnp.where(hit, base[:, 0:1] + before, 0.0), axis=0, keepdims=True))
        base = base + jnp.sum(onehot, axis=1, keepdims=True)
    rank_o[0] = jnp.concatenate(ranks, axis=0).astype(jnp.int32)
    base_scr[...] = base
    cnt_o[...] = base


def _merge(oa, oc, bu, bv, g, x, gate1, shift2, scale2, lw, sgu_w, sgu_b, counts, tm):
    b, s, d = x.shape
    r = gate1.shape[1]
    bw = bu.shape[-1]
    n_tiles = b * (s // tm)
    tok = lambda w: pl.BlockSpec((1, tm, w), lambda bi, i: (bi, i, 0))
    mod = pl.BlockSpec((1, r, d), lambda bi, i: (bi, 0, 0))
    per_tile = pl.BlockSpec((1, TOP_K, tm), lambda bi, i: (bi * (s // tm) + i, 0, 0))
    tri = jnp.asarray(np.triu(np.ones((tm, tm), np.float32), k=1), BF16)
    in_specs = [tok(bw), tok(bw), tok(bw), tok(bw), tok(N_BRANCH * d), tok(d), mod, mod, mod,
                _const_spec((1, d)), _const_spec(lw["w_br"].shape), _const_spec((d, d)),
                _const_spec(sgu_w.shape), _const_spec(sgu_b.shape),
                _const_spec((2, N_EXPERTS, d)), _const_spec((N_EXPERTS, 1)),
                _const_spec((tm, tm)), _const_spec((N_EXPERTS, 128))]
    out_shape = [jax.ShapeDtypeStruct((b, s, d), F32), jax.ShapeDtypeStruct((b, s, d), F32),
                 jax.ShapeDtypeStruct((n_tiles, TOP_K, tm), jnp.int32),
                 jax.ShapeDtypeStruct((n_tiles, TOP_K, tm), F32),
                 jax.ShapeDtypeStruct((n_tiles, TOP_K, tm), jnp.int32),
                 jax.ShapeDtypeStruct((N_EXPERTS, 128), F32)]
    out_specs = [tok(d), tok(d), per_tile, per_tile, per_tile, _const_spec((N_EXPERTS, 128))]
    return pl.pallas_call(
        _merge_kernel,
        grid=(b, s // tm),
        in_specs=in_specs,
        out_specs=out_specs,
        out_shape=out_shape,
        scratch_shapes=[pltpu.VMEM((N_EXPERTS, 128), F32)],
        compiler_params=_cparams(2, VMEM_LIMIT_BYTES),
    )(oa, oc, bu, bv, g, x, gate1, shift2, scale2, lw["g_ffn"], lw["w_br"], lw["w_out"],
      sgu_w, sgu_b, lw["w_rt"], lw["b_r"], tri, counts)


def _row_copy(src_ref, src_row, dst_ref, dst_row, sem):
    return pltpu.make_async_copy(src_ref.at[pl.ds(src_row, 1)], dst_ref.at[pl.ds(dst_row, 1)], sem)


def _dispatch_kernel(pos_ref, h_ref, xs_in_ref, xs_ref, sem):
    del xs_in_ref
    tm = h_ref.shape[0]
    base = pl.program_id(0) * (tm * TOP_K)

    for t in range(tm):
        for k in range(TOP_K):
            _row_copy(h_ref, t, xs_ref, pos_ref[base + t * TOP_K + k], sem).start()
    for _ in range(tm * TOP_K):
        _row_copy(h_ref, 0, xs_ref, 0, sem).wait()


def _dispatch(pos_flat, h, xs, tm):
    t, d = h.shape
    return pl.pallas_call(
        _dispatch_kernel,
        grid_spec=pltpu.PrefetchScalarGridSpec(
            num_scalar_prefetch=1,
            grid=(t // tm,),
            in_specs=[pl.BlockSpec((tm, d), lambda i, pos: (i, 0)),
                      pl.BlockSpec(memory_space=pl.ANY)],
            out_specs=pl.BlockSpec(memory_space=pl.ANY),
            scratch_shapes=[pltpu.SemaphoreType.DMA]),
        out_shape=jax.ShapeDtypeStruct(xs.shape, xs.dtype),
        input_output_aliases={2: 0},
        compiler_params=_cparams(1),
    )(pos_flat, h, xs)


def _expert_kernel(be_ref, nused_ref, xs_ref, wgu_ref, bgu_ref, wdn_ref, bdn_ref, ys_ref,
                   wgu_scr, wdn_scr):
    i = pl.program_id(0)
    prev_e = be_ref[jnp.maximum(i - 1, 0)]
    new_expert = jnp.logical_or(i == 0, be_ref[i] != prev_e)
    active = i < nused_ref[0]

    @pl.when(jnp.logical_and(active, new_expert))
    def _():
        wgu_scr[...] = wgu_ref[0].astype(BF16)
        wdn_scr[...] = wdn_ref[0].astype(BF16)

    @pl.when(active)
    def _():
        de = wdn_scr.shape[0]
        gu = jnp.dot(xs_ref[...].astype(BF16), wgu_scr[...],
                     preferred_element_type=F32) + bgu_ref[0]
        g = jnp.minimum(gu[:, :de], SWIGLU_LIMIT)
        u = jnp.clip(gu[:, de:], -SWIGLU_LIMIT, SWIGLU_LIMIT)
        act = (u + 1.0) * g * jax.nn.sigmoid(SWIGLU_ALPHA * g)
        ys_ref[...] = jnp.dot(act.astype(BF16), wdn_scr[...],
                              preferred_element_type=F32) + bdn_ref[0]

    @pl.when(jnp.logical_not(active))
    def _():
        ys_ref[...] = jnp.zeros_like(ys_ref)


def _experts(block_e, n_used, xs, w_gu, b_gu, w_dn, b_dn, rows):
    n_rows, d = xs.shape
    n_e, _, two_de = w_gu.shape
    de = two_de // 2
    n_blocks = n_rows // rows
    row_map = lambda i, be, nu: (jnp.minimum(i, nu[0] - 1), 0)
    e_map = lambda i, be, nu: (be[jnp.minimum(i, nu[0] - 1)], 0, 0)
    return pl.pallas_call(
        _expert_kernel,
        grid_spec=pltpu.PrefetchScalarGridSpec(
            num_scalar_prefetch=2,
            grid=(n_blocks,),
            in_specs=[pl.BlockSpec((rows, d), row_map),
                      pl.BlockSpec((1, d, two_de), e_map),
                      pl.BlockSpec((1, 1, two_de), e_map),
                      pl.BlockSpec((1, de, d), e_map),
                      pl.BlockSpec((1, 1, d), e_map)],
            out_specs=pl.BlockSpec((rows, d), lambda i, be, nu: (i, 0)),
            scratch_shapes=[pltpu.VMEM((d, two_de), BF16), pltpu.VMEM((de, d), BF16)]),
        out_shape=jax.ShapeDtypeStruct((n_rows, d), F32),
        compiler_params=_cparams(1, VMEM_LIMIT_BYTES),
    )(block_e, n_used, xs, w_gu, b_gu.reshape(n_e, 1, two_de), w_dn, b_dn.reshape(n_e, 1, d))


def _combine_kernel(pos_ref, ys_ref, gates_ref, x_ref, gate2_ref, o_ref, buf, sem):
    tm = x_ref.shape[1]
    n_i = pl.num_programs(1)
    base = (pl.program_id(0) * n_i + pl.program_id(1)) * (tm * TOP_K)

    for t in range(tm):
        for k in range(TOP_K):
            _row_copy(ys_ref, pos_ref[base + t * TOP_K + k], buf.at[k], t, sem).start()
    for _ in range(tm * TOP_K):
        _row_copy(ys_ref, 0, buf.at[0], 0, sem).wait()

    gates = gates_ref[0]
    ff = gates[:, 0:1] * buf[0]
    for k in range(1, TOP_K):
        ff = ff + gates[:, k:k + 1] * buf[k]
    o_ref[0] = x_ref[0] + gate2_ref[0] * ff


def _combine(pos_flat, ys, gates_tk, x, gate2, tm):
    b, s, d = x.shape
    r = gate2.shape[1]
    return pl.pallas_call(
        _combine_kernel,
        grid_spec=pltpu.PrefetchScalarGridSpec(
            num_scalar_prefetch=1,
            grid=(b, s // tm),
            in_specs=[pl.BlockSpec(memory_space=pl.ANY),
                      pl.BlockSpec((1, tm, TOP_K), lambda bi, i, pos: (bi, i, 0)),
                      pl.BlockSpec((1, tm, d), lambda bi, i, pos: (bi, i, 0)),
                      pl.BlockSpec((1, r, d), lambda bi, i, pos: (bi, 0, 0))],
            out_specs=pl.BlockSpec((1, tm, d), lambda bi, i, pos: (bi, i, 0)),
            scratch_shapes=[pltpu.VMEM((TOP_K, tm, d), F32), pltpu.SemaphoreType.DMA]),
        out_shape=jax.ShapeDtypeStruct((b, s, d), F32),
        compiler_params=_cparams(2, VMEM_LIMIT_BYTES),
    )(pos_flat, ys, gates_tk, x, gate2)


def _route(ids, ranks, counts, rows, n_blocks):
    padded = (counts + rows - 1) // rows * rows
    pad_end = jnp.cumsum(padded)
    pad_start = pad_end - padded
    experts = jnp.arange(N_EXPERTS, dtype=jnp.int32)
    pos = jnp.sum(jnp.where(ids[..., None] == experts, pad_start, 0), axis=-1) + ranks
    first_row = jnp.arange(n_blocks, dtype=jnp.int32) * rows
    block_e = jnp.minimum(jnp.sum((pad_end[None, :] <= first_row[:, None]).astype(jnp.int32), axis=1),
                          N_EXPERTS - 1)
    n_used = jnp.maximum(pad_end[-1:] // rows, 1).astype(jnp.int32)
    return pos.astype(jnp.int32), block_e, n_used


def _tiles_to_tokens(a):
    n_tiles, k, tm = a.shape
    return jnp.transpose(a, (1, 0, 2)).reshape(k, n_tiles * tm)


def _split_hi_lo(w):
    hi = w.astype(BF16)
    return jnp.stack([hi, (w - hi.astype(F32)).astype(BF16)])


def _layer_weights(l, w_in, g_norm_mix, g_norm_ffn, a_q_norm, a_k_norm, b_ln_g, b_ln_b,
                   c_q_norm, c_k_norm, w_branch, w_out, w_router, b_router, widths):
    d = w_in.shape[1]
    aw, kvw = widths[0], widths[6]
    splits = np.cumsum(widths)[:-1].tolist()
    w_split = [w.astype(BF16) for w in jnp.split(w_in[l], splits, axis=-1)]
    head = np.arange(aw) // HEAD_DIM
    bd = jnp.asarray(head[:, None] == head[None, :], BF16)
    tile = lambda v, w: jnp.tile(v, w // HEAD_DIM).reshape(1, w)
    return dict(
        w_split=w_split, bd=bd,
        g_mix=g_norm_mix[l].reshape(1, d), g_ffn=g_norm_ffn[l].reshape(1, d),
        aqn=tile(a_q_norm[l], aw), akn=tile(a_k_norm[l], aw),
        cqn=tile(c_q_norm[l], aw), ckn=tile(c_k_norm[l], kvw),
        lng=b_ln_g[l].reshape(1, aw), lnb=b_ln_b[l].reshape(1, aw),
        w_br=w_branch[l].astype(BF16), w_out=w_out[l].astype(BF16),
        w_rt=_split_hi_lo(jnp.transpose(w_router[l])), b_r=b_router[l].reshape(N_EXPERTS, 1))


def kernel(x_prompt, x_sample, cache_a_k, cache_a_v, cache_c_k, cache_c_v, c_prompt, c_sample,
           w_ada, b_ada, g_norm_mix, g_norm_ffn, w_in, a_q_norm, a_k_norm, a_rel_bias,
           b_ln_g, b_ln_b, b_w_s, b_b_s, c_q_norm, c_k_norm, c_sinks, t5_bias,
           w_branch, w_out, w_router, b_router, w_gate_up, b_gate_up, w_down, b_down):
    bp, sp, d = x_prompt.shape
    bs, ss, _ = x_sample.shape
    depth = w_in.shape[0]
    ts = bs * ss
    tp = bp * sp
    a_heads, c_kv_heads = cache_a_k.shape[3], cache_c_k.shape[3]
    aw = a_heads * HEAD_DIM
    kvw = c_kv_heads * HEAD_DIM
    widths = [aw, aw, aw, aw, aw, aw, kvw, kvw, N_BRANCH * d]
    ra, rc = cache_a_k.shape[2], cache_c_k.shape[2]
    a_band, c_band = A_PREV_CHUNKS * CHUNK, C_PREV_CHUNKS * CHUNK
    proj_tile, merge_tile = min(PROJ_TILE, sp), min(MERGE_TILE, sp)
    a_tile, c_tile = min(A_Q_TILE, sp), min(C_Q_TILE, sp)
    moe_tile = min(MOE_TOKEN_TILE, tp)

    c_rows = jnp.concatenate([c_prompt, c_sample,
                              jnp.zeros((ADA_ROWS - bp - bs, d), F32)], axis=0)
    mod_all = _ada_modulation(c_rows, w_ada, b_ada).reshape(depth, ADA_ROWS, 6, d)

    t5_p = _c_bias_table(t5_bias, c_band, CHUNK, c_band + CHUNK)
    t5_s = _c_bias_table(t5_bias, rc, ss, rc + ss)
    bias_c_p = _band_bias(t5_p, c_tile, C_PREV_CHUNKS)

    n_assign = (tp + ts) * TOP_K
    n_blocks = -(-(n_assign + N_EXPERTS * (MOE_ROWS - 1)) // MOE_ROWS)
    n_rows = n_blocks * MOE_ROWS

    mask = (np.arange(B_CHUNK)[None, :] // CHUNK <= np.arange(B_CHUNK)[:, None] // CHUNK)
    mask = jnp.asarray(mask, F32)

    xp = x_prompt
    xs = x_sample.reshape(1, ts, d)
    st_p, st_s = [], []
    for l in range(depth):
        lw = _layer_weights(l, w_in, g_norm_mix, g_norm_ffn, a_q_norm, a_k_norm, b_ln_g, b_ln_b,
                            c_q_norm, c_k_norm, w_branch, w_out, w_router, b_router, widths)
        mod_p = mod_all[l, :bp]
        mod_s = jnp.repeat(mod_all[l, bp:bp + bs], ss, axis=0)
        mp = [mod_p[:, i].reshape(bp, 1, d) for i in range(6)]
        ms = [mod_s[:, i].reshape(1, ts, d) for i in range(6)]

        ws_m = b_w_s[l] * mask
        sgu_w_p = ws_m.astype(BF16)
        sgu_b_p = jnp.repeat(jnp.transpose(b_b_s[l]), aw // B_GROUPS, axis=1)
        eye = jnp.eye(B_CHUNK // ss, dtype=F32)
        sgu_w_s = jnp.stack([jnp.kron(eye, ws_m[g, :ss, :ss]) for g in range(B_GROUPS)]).astype(BF16)
        sgu_b_s = jnp.tile(sgu_b_p[:ss], (B_CHUNK // ss, 1))

        bias_a_p = _band_bias(_a_bias_table(a_rel_bias[l], a_band, CHUNK, a_band + CHUNK),
                              a_tile, A_PREV_CHUNKS)
        bias_a_s = _a_bias_table(a_rel_bias[l], ra, ss, ra + ss)
        sinks = c_sinks[l].astype(F32)

        aq, ak, av, bu, bv, cq, ck, cv, g = _project(xp, mp[0], mp[1], lw, proj_tile)
        oa = _band_attention(aq, ak, av, bias_a_p, None, a_band, a_tile)
        oc = _band_attention(cq, ck, cv, bias_c_p, sinks, c_band, c_tile)
        na, nc = min(a_band, sp), min(c_band, sp)
        st_p.append((ak[:, sp - na:].reshape(bp, na, a_heads, HEAD_DIM),
                     av[:, sp - na:].reshape(bp, na, a_heads, HEAD_DIM),
                     ck[:, sp - nc:].reshape(bp, nc, c_kv_heads, HEAD_DIM),
                     cv[:, sp - nc:].reshape(bp, nc, c_kv_heads, HEAD_DIM)))
        zero_counts = jnp.zeros((N_EXPERTS, 128), F32)
        xp_mid, h2_p, ids_p, gates_p, rank_p, counts_p = _merge(
            oa, oc, bu, bv, g, xp, mp[2], mp[3], mp[4], lw, sgu_w_p, sgu_b_p, zero_counts,
            merge_tile)

        aq, ak, av, bu, bv, cq, ck, cv, g = _project(xs, ms[0], ms[1], lw, ts)
        per_seq = lambda a: a.reshape(bs, ss, a.shape[-1])
        oa = _band_attention(per_seq(aq), per_seq(ak), per_seq(av), bias_a_s, None, ra, ss,
                             prev=(cache_a_k[l].reshape(bs, ra, aw), cache_a_v[l].reshape(bs, ra, aw)))
        oc = _band_attention(per_seq(cq), per_seq(ck), per_seq(cv), t5_s, sinks, rc, ss,
                             prev=(cache_c_k[l].reshape(bs, rc, kvw), cache_c_v[l].reshape(bs, rc, kvw)))
        st_s.append((ak.reshape(bs, ss, a_heads, HEAD_DIM), av.reshape(bs, ss, a_heads, HEAD_DIM),
                     ck.reshape(bs, ss, c_kv_heads, HEAD_DIM), cv.reshape(bs, ss, c_kv_heads, HEAD_DIM),
                     bv.reshape(bs, ss, aw)))
        xs_mid, h2_s, ids_s, gates_s, rank_s, counts = _merge(
            oa.reshape(1, ts, aw), oc.reshape(1, ts, aw), bu, bv, g, xs, ms[2], ms[3], ms[4], lw,
            sgu_w_s, sgu_b_s, counts_p, ts)

        ids = jnp.concatenate([_tiles_to_tokens(ids_p), _tiles_to_tokens(ids_s)], axis=1)
        ranks = jnp.concatenate([_tiles_to_tokens(rank_p), _tiles_to_tokens(rank_s)], axis=1)
        pos, block_e, n_used = _route(ids, ranks, counts[:, 0].astype(jnp.int32), MOE_ROWS, n_blocks)
        pos_tk = jnp.transpose(pos)
        pos_p = pos_tk[:tp].reshape(-1)
        pos_s = pos_tk[tp:].reshape(-1)
        rows_in = jnp.zeros((n_rows, d), F32)
        rows_in = _dispatch(pos_p, h2_p.reshape(tp, d), rows_in, moe_tile)
        rows_in = _dispatch(pos_s, h2_s.reshape(ts, d), rows_in, ts)
        n_e = w_gate_up.shape[1]
        rows_out = _experts(block_e + l * n_e, n_used, rows_in,
                            w_gate_up.reshape((depth * n_e,) + w_gate_up.shape[2:]),
                            b_gate_up.reshape(depth * n_e, -1),
                            w_down.reshape((depth * n_e,) + w_down.shape[2:]),
                            b_down.reshape(depth * n_e, -1), MOE_ROWS)
        gates_p_tk = jnp.transpose(_tiles_to_tokens(gates_p)).reshape(bp, sp, TOP_K)
        gates_s_tk = jnp.transpose(_tiles_to_tokens(gates_s)).reshape(1, ts, TOP_K)
        xp = _combine(pos_p, rows_out, gates_p_tk, xp_mid, mp[5], moe_tile)
        xs = _combine(pos_s, rows_out, gates_s_tk, xs_mid, ms[5], ts)

    stack = lambda rows, i: jnp.stack([r[i] for r in rows], axis=0)
    return (xp, xs.reshape(bs, ss, d),
            stack(st_p, 0), stack(st_p, 1), stack(st_p, 2), stack(st_p, 3),
            stack(st_s, 0), stack(st_s, 1), stack(st_s, 2), stack(st_s, 3), stack(st_s, 4))
```

```python
import functools
import math

import jax
import jax.numpy as jnp
import numpy as np
from jax import lax
from jax.experimental import pallas as pl
from jax.experimental.pallas import tpu as pltpu

F32 = jnp.float32
BF16 = jnp.bfloat16

CHUNK = 64
HEAD_DIM = 64
A_PREV_CHUNKS = 8
A_REL_CLIP = 128
B_CHUNK = 128
B_GROUPS = 4
C_PREV_CHUNKS = 2
T5_BUCKETS = 32
T5_MAX_DIST = 128
N_BRANCH = 3
N_EXPERTS = 32
TOP_K = 4
SWIGLU_ALPHA = 1.702
SWIGLU_LIMIT = 7.0
RMS_EPS = 1e-6
LN_EPS = 1e-5
NEG_INF = -1e30
F32_LOWEST = float(np.finfo(np.float32).min)

VMEM_LIMIT_BYTES = 56 * 1024 * 1024
ADA_ROWS = 16
PROJ_TILE = 512
MERGE_TILE = 256
A_Q_TILE = 256
C_Q_TILE = 128
MOE_ROWS = 512
MOE_TOKEN_TILE = 256


def _cparams(n_axes, vmem=None):
    return pltpu.CompilerParams(dimension_semantics=("arbitrary",) * n_axes,
                                vmem_limit_bytes=vmem)


def _const_spec(shape):
    nd = len(shape)
    return pl.BlockSpec(shape, lambda *_: (0,) * nd, pipeline_mode=pl.Buffered(1))


def _ada_kernel(c_ref, w_ref, b_ref, o_ref):
    c = c_ref[...]
    s = c * jax.nn.sigmoid(c)
    o_ref[0] = jnp.dot(s, w_ref[0], preferred_element_type=F32,
                       precision=lax.Precision.HIGHEST) + b_ref[0]


def _ada_modulation(c_rows, w_ada, b_ada):
    depth, d, _ = w_ada.shape
    return pl.pallas_call(
        _ada_kernel,
        grid=(depth, 6),
        in_specs=[pl.BlockSpec((ADA_ROWS, d), lambda l, j: (0, 0)),
                  pl.BlockSpec((1, d, d), lambda l, j: (l, 0, j)),
                  pl.BlockSpec((1, 1, d), lambda l, j: (l, 0, j))],
        out_specs=pl.BlockSpec((1, ADA_ROWS, d), lambda l, j: (l, 0, j)),
        out_shape=jax.ShapeDtypeStruct((depth, ADA_ROWS, 6 * d), F32),
        compiler_params=_cparams(2),
    )(c_rows, w_ada, b_ada.reshape(depth, 1, 6 * d))


def _row_rms(x, gain):
    return x * lax.rsqrt(jnp.mean(x * x, axis=-1, keepdims=True) + RMS_EPS) * gain


def _head_rms(t, gain, bd):
    ss = jnp.dot((t * t).astype(BF16), bd, preferred_element_type=F32)
    return t * lax.rsqrt(ss * (1.0 / HEAD_DIM) + RMS_EPS) * gain


def _proj_kernel(x_ref, shift_ref, scale_ref, gmix_ref,
                 w_aq, w_ak, w_av, w_bu, w_bv, w_cq, w_ck, w_cv, w_g,
                 bd_ref, aqn, akn, cqn, ckn, lng, lnb,
                 aq_o, ak_o, av_o, bu_o, bv_o, cq_o, ck_o, cv_o, g_o):
    x = x_ref[0]
    h = _row_rms(x, gmix_ref[...]) * (1.0 + scale_ref[0]) + shift_ref[0]
    hb = h.astype(BF16)

    def mm(w_ref):
        return jnp.dot(hb, w_ref[...], preferred_element_type=F32)

    bd = bd_ref[...]
    q_scale = HEAD_DIM ** -0.5
    aq_o[0] = (_head_rms(mm(w_aq), aqn[...], bd) * q_scale).astype(BF16)
    ak_o[0] = _head_rms(mm(w_ak), akn[...], bd)
    av_o[0] = mm(w_av)
    bu_o[0] = jax.nn.gelu(mm(w_bu)).astype(BF16)
    bv = jax.nn.gelu(mm(w_bv))
    mu = jnp.mean(bv, axis=-1, keepdims=True)
    var = jnp.mean(jnp.square(bv - mu), axis=-1, keepdims=True)
    bv_o[0] = (bv - mu) * lax.rsqrt(var + LN_EPS) * lng[...] + lnb[...]
    cq_o[0] = (_head_rms(mm(w_cq), cqn[...], bd) * q_scale).astype(BF16)
    kvw = ck_o.shape[-1]
    ck_o[0] = _head_rms(mm(w_ck), ckn[...], bd[:kvw, :kvw])
    cv_o[0] = mm(w_cv)
    d = x.shape[-1]
    for r in range(N_BRANCH):
        g_o[0, :, r * d:(r + 1) * d] = jax.nn.sigmoid(
            jnp.dot(hb, w_g[:, r * d:(r + 1) * d], preferred_element_type=F32)).astype(BF16)


def _project(x, shift, scale, lw, tm):
    b, s, d = x.shape
    r = shift.shape[1]
    widths = [w.shape[1] for w in lw["w_split"]]
    aw, kvw = widths[0], widths[6]
    tok = lambda w: pl.BlockSpec((1, tm, w), lambda bi, i: (bi, i, 0))
    mod = pl.BlockSpec((1, r, d), lambda bi, i: (bi, 0, 0))
    in_specs = ([tok(d), mod, mod, _const_spec((1, d))]
                + [_const_spec(w.shape) for w in lw["w_split"]]
                + [_const_spec(lw["bd"].shape)]
                + [_const_spec((1, aw)), _const_spec((1, aw)), _const_spec((1, aw)),
                   _const_spec((1, kvw)), _const_spec((1, aw)), _const_spec((1, aw))])
    out_dt = [BF16, F32, F32, BF16, F32, BF16, F32, F32, BF16]
    return pl.pallas_call(
        _proj_kernel,
        grid=(b, s // tm),
        in_specs=in_specs,
        out_specs=[tok(w) for w in widths],
        out_shape=[jax.ShapeDtypeStruct((b, s, w), dt) for w, dt in zip(widths, out_dt)],
        compiler_params=_cparams(2, VMEM_LIMIT_BYTES),
    )(x, shift, scale, lw["g_mix"], *lw["w_split"], lw["bd"],
      lw["aqn"], lw["akn"], lw["cqn"], lw["ckn"], lw["lng"], lw["lnb"])


def _attn_kernel(*refs, n_q_heads, n_kv_heads, has_sink, carry):
    refs = list(refs)
    q_ref, k_ref, v_ref = refs[:3]
    pos = 3
    if not carry:
        pk_ref, pv_ref = refs[pos:pos + 2]
        pos += 2
    bias_ref = refs[pos]
    pos += 1
    if has_sink:
        sink_ref = refs[pos]
        pos += 1
    o_ref = refs[pos]
    pos += 1
    if carry:
        pk_scr, pv_scr = refs[pos:pos + 2]

    j = pl.program_id(1)
    kc = k_ref[0].astype(BF16)
    vc = v_ref[0].astype(BF16)
    tq = kc.shape[0]
    if carry:
        @pl.when(j == 0)
        def _():
            pk_scr[...] = jnp.zeros_like(pk_scr)
            pv_scr[...] = jnp.zeros_like(pv_scr)
        kp, vp = pk_scr[...], pv_scr[...]
    else:
        kp, vp = pk_ref[0].astype(BF16), pv_ref[0].astype(BF16)
    n_prev = kp.shape[0]
    kw = jnp.concatenate([kp, kc], axis=0)
    vw = jnp.concatenate([vp, vc], axis=0)
    width = n_prev + tq
    if carry:
        before_start = lax.broadcasted_iota(jnp.int32, (tq, width), 1) < (n_prev - j * tq)
    q = q_ref[0]
    group = n_q_heads // n_kv_heads
    outs = []
    for h in range(n_q_heads):
        kh = h // group
        qh = q[:, h * HEAD_DIM:(h + 1) * HEAD_DIM]
        s = lax.dot_general(qh, kw[:, kh * HEAD_DIM:(kh + 1) * HEAD_DIM],
                            (((1,), (1,)), ((), ())), preferred_element_type=F32) + bias_ref[h]
        if carry:
            s = jnp.where(before_start, NEG_INF, s)
        m = jnp.max(s, axis=-1, keepdims=True)
        if has_sink:
            sink = sink_ref[h]
            m = jnp.maximum(m, sink)
        e = jnp.exp(s - m)
        denom = jnp.sum(e, axis=-1, keepdims=True)
        if has_sink:
            denom = denom + jnp.exp(sink - m)
        o = jnp.dot(e.astype(BF16), vw[:, kh * HEAD_DIM:(kh + 1) * HEAD_DIM],
                    preferred_element_type=F32)
        outs.append(o / denom)
    o_ref[0] = jnp.concatenate(outs, axis=-1).astype(BF16)
    if carry:
        pk_scr[...] = kw[tq:, :]
        pv_scr[...] = vw[tq:, :]


def _band_attention(q, k, v, bias, sinks, n_prev_rows, tq, prev=None):
    b, s, qw = q.shape
    kvw = k.shape[-1]
    hq, kvh = qw // HEAD_DIM, kvw // HEAD_DIM
    carry = prev is None
    width = n_prev_rows + tq
    tok = lambda w: pl.BlockSpec((1, tq, w), lambda bi, i: (bi, i, 0))
    in_specs = [tok(qw), tok(kvw), tok(kvw)]
    args = [q, k, v]
    if not carry:
        pspec = pl.BlockSpec((1, n_prev_rows, kvw), lambda bi, i: (bi, 0, 0))
        in_specs += [pspec, pspec]
        args += list(prev)
    in_specs.append(_const_spec((hq, tq, width)))
    args.append(bias)
    if sinks is not None:
        in_specs.append(pl.BlockSpec(memory_space=pltpu.SMEM))
        args.append(sinks)
    scratch = [pltpu.VMEM((n_prev_rows, kvw), BF16)] * 2 if carry else []
    return pl.pallas_call(
        functools.partial(_attn_kernel, n_q_heads=hq, n_kv_heads=kvh,
                          has_sink=sinks is not None, carry=carry),
        grid=(b, s // tq),
        in_specs=in_specs,
        out_specs=tok(qw),
        out_shape=jax.ShapeDtypeStruct((b, s, qw), BF16),
        scratch_shapes=scratch,
        compiler_params=_cparams(2, VMEM_LIMIT_BYTES),
    )(*args)


def _band_bias(base, tq, n_prev_chunks):
    n_qc = tq // CHUNK
    blocks = [jnp.pad(base, ((0, 0), (0, 0), (jq * CHUNK, (n_qc - 1 - jq) * CHUNK)),
                      constant_values=NEG_INF) for jq in range(n_qc)]
    return jnp.concatenate(blocks, axis=1)


def _toeplitz(vec, n_q, n_k):
    return jnp.stack([vec[:, n_q - 1 - i:n_q - 1 - i + n_k] for i in range(n_q)], axis=1)


def _a_bias_table(a_rel, q0, n_q, n_k):
    t = np.arange(n_q + n_k - 1)
    idx = np.clip(q0 + n_q - 1 - t, -A_REL_CLIP, A_REL_CLIP) + A_REL_CLIP
    return _toeplitz(a_rel[:, idx].astype(F32), n_q, n_k)


def _t5_bucket(rel):
    n = T5_BUCKETS // 2
    max_exact = n // 2
    ret = jnp.where(rel > 0, n, 0)
    a = jnp.abs(rel)
    large = max_exact + (jnp.log(jnp.maximum(a, 1).astype(F32) / max_exact)
                         / math.log(T5_MAX_DIST / max_exact) * (n - max_exact)).astype(jnp.int32)
    large = jnp.minimum(large, n - 1)
    return ret + jnp.where(a < max_exact, a, large)


def _c_bias_table(t5_table, q0, n_q, n_k):
    rel = jnp.arange(n_q + n_k - 1, dtype=jnp.int32) - (n_q - 1) - q0
    vec = jnp.transpose(t5_table[_t5_bucket(rel)]).astype(F32)
    return _toeplitz(vec, n_q, n_k)


def _merge_kernel(oa_ref, oc_ref, bu_ref, bv_ref, g_ref, x_ref, gate1_ref, shift2_ref, scale2_ref,
                  gffn_ref, wbr_ref, wout_ref, ws_ref, bsb_ref, wrt_ref, br_ref, tri_ref, cnt_ref,
                  xnew_o, h2_o, ids_o, gates_o, rank_o, cnt_o, base_scr):
    first = jnp.logical_and(pl.program_id(0) == 0, pl.program_id(1) == 0)

    @pl.when(first)
    def _():
        base_scr[...] = cnt_ref[...]

    tm, d = x_ref.shape[1], x_ref.shape[2]
    bw = bv_ref.shape[2]
    gch = bw // B_GROUPS
    bvb = bv_ref[0].astype(BF16)
    rows = []
    for r in range(tm // B_CHUNK):
        cols = [jnp.dot(ws_ref[gi], bvb[r * B_CHUNK:(r + 1) * B_CHUNK, gi * gch:(gi + 1) * gch],
                        preferred_element_type=F32) for gi in range(B_GROUPS)]
        rows.append(jnp.concatenate(cols, axis=-1) + bsb_ref[...])
    mixed = jnp.concatenate(rows, axis=0) if len(rows) > 1 else rows[0]
    ob = (bu_ref[0].astype(F32) * mixed).astype(BF16)

    branches = (oa_ref[0], ob, oc_ref[0])
    y = None
    for r in range(N_BRANCH):
        br = jnp.dot(branches[r], wbr_ref[r], preferred_element_type=F32)
        term = g_ref[0, :, r * d:(r + 1) * d].astype(F32) * br
        y = term if y is None else y + term
    out = jnp.dot(y.astype(BF16), wout_ref[...], preferred_element_type=F32)
    xn = x_ref[0] + gate1_ref[0] * out
    xnew_o[0] = xn
    h2 = _row_rms(xn, gffn_ref[...]) * (1.0 + scale2_ref[0]) + shift2_ref[0]
    h2_o[0] = h2

    nt = (((1,), (1,)), ((), ()))
    h_hi = h2.astype(BF16)
    h_lo = (h2 - h_hi.astype(F32)).astype(BF16)
    logits = (lax.dot_general(wrt_ref[0], h_hi, nt, preferred_element_type=F32)
              + lax.dot_general(wrt_ref[0], h_lo, nt, preferred_element_type=F32)
              + lax.dot_general(wrt_ref[1], h_hi, nt, preferred_element_type=F32)) + br_ref[...]
    n_exp = logits.shape[0]
    eidx = lax.broadcasted_iota(jnp.int32, (n_exp, tm), 0).astype(F32)
    vals, ids = [], []
    rest = logits
    for _ in range(TOP_K):
        m = jnp.max(rest, axis=0, keepdims=True)
        sel = jnp.min(jnp.where(rest == m, eidx, float(n_exp)), axis=0, keepdims=True)
        vals.append(m)
        ids.append(sel)
        rest = jnp.where(eidx == sel, F32_LOWEST, rest)
    exps = [jnp.exp(v - vals[0]) for v in vals]
    tot = exps[0] + exps[1] + exps[2] + exps[3]
    gates_o[0] = jnp.concatenate([e / tot for e in exps], axis=0)
    ids_o[0] = jnp.concatenate(ids, axis=0).astype(jnp.int32)

    base = base_scr[...]
    ranks = []
    for k in range(TOP_K):
        hit = eidx == ids[k]
        onehot = jnp.where(hit, 1.0, 0.0)
        before = jnp.dot(onehot.astype(BF16), tri_ref[...], preferred_element_type=F32)
        ranks.append(jnp.sum(jnp.where(hit, base[:, 0:1] + before, 0.0), axis=0, keepdims=True))
        base = base + jnp.sum(onehot, axis=1, keepdims=True)
    rank_o[0] = jnp.concatenate(ranks, axis=0).astype(jnp.int32)
    base_scr[...] = base
    cnt_o[...] = base


def _merge(oa, oc, bu, bv, g, x, gate1, shift2, scale2, lw, sgu_w, sgu_b, counts, tm):
    b, s, d = x.shape
    r = gate1.shape[1]
    bw = bu.shape[-1]
    n_tiles = b * (s // tm)
    tok = lambda w: pl.BlockSpec((1, tm, w), lambda bi, i: (bi, i, 0))
    mod = pl.BlockSpec((1, r, d), lambda bi, i: (bi, 0, 0))
    per_tile = pl.BlockSpec((1, TOP_K, tm), lambda bi, i: (bi * (s // tm) + i, 0, 0))
    tri = jnp.asarray(np.triu(np.ones((tm, tm), np.float32), k=1), BF16)
    in_specs = [tok(bw), tok(bw), tok(bw), tok(bw), tok(N_BRANCH * d), tok(d), mod, mod, mod,
                _const_spec((1, d)), _const_spec(lw["w_br"].shape), _const_spec((d, d)),
                _const_spec(sgu_w.shape), _const_spec(sgu_b.shape),
                _const_spec((2, N_EXPERTS, d)), _const_spec((N_EXPERTS, 1)),
                _const_spec((tm, tm)), _const_spec((N_EXPERTS, 128))]
    out_shape = [jax.ShapeDtypeStruct((b, s, d), F32), jax.ShapeDtypeStruct((b, s, d), F32),
                 jax.ShapeDtypeStruct((n_tiles, TOP_K, tm), jnp.int32),
                 jax.ShapeDtypeStruct((n_tiles, TOP_K, tm), F32),
                 jax.ShapeDtypeStruct((n_tiles, TOP_K, tm), jnp.int32),
                 jax.ShapeDtypeStruct((N_EXPERTS, 128), F32)]
    out_specs = [tok(d), tok(d), per_tile, per_tile, per_tile, _const_spec((N_EXPERTS, 128))]
    return pl.pallas_call(
        _merge_kernel,
        grid=(b, s // tm),
        in_specs=in_specs,
        out_specs=out_specs,
        out_shape=out_shape,
        scratch_shapes=[pltpu.VMEM((N_EXPERTS, 128), F32)],
        compiler_params=_cparams(2, VMEM_LIMIT_BYTES),
    )(oa, oc, bu, bv, g, x, gate1, shift2, scale2, lw["g_ffn"], lw["w_br"], lw["w_out"],
      sgu_w, sgu_b, lw["w_rt"], lw["b_r"], tri, counts)


def _row_copy(src_ref, src_row, dst_ref, dst_row, sem):
    return pltpu.make_async_copy(src_ref.at[pl.ds(src_row, 1)], dst_ref.at[pl.ds(dst_row, 1)], sem)


def _dispatch_kernel(pos_ref, h_ref, xs_in_ref, xs_ref, sem):
    del xs_in_ref
    tm = h_ref.shape[0]
    base = pl.program_id(0) * (tm * TOP_K)

    for t in range(tm):
        for k in range(TOP_K):
            _row_copy(h_ref, t, xs_ref, pos_ref[base + t * TOP_K + k], sem).start(priority=k % 2)
    for _ in range(tm * TOP_K):
        _row_copy(h_ref, 0, xs_ref, 0, sem).wait()


def _dispatch(pos_flat, h, xs, tm):
    t, d = h.shape
    return pl.pallas_call(
        _dispatch_kernel,
        grid_spec=pltpu.PrefetchScalarGridSpec(
            num_scalar_prefetch=1,
            grid=(t // tm,),
            in_specs=[pl.BlockSpec((tm, d), lambda i, pos: (i, 0)),
                      pl.BlockSpec(memory_space=pl.ANY)],
            out_specs=pl.BlockSpec(memory_space=pl.ANY),
            scratch_shapes=[pltpu.SemaphoreType.DMA]),
        out_shape=jax.ShapeDtypeStruct(xs.shape, xs.dtype),
        input_output_aliases={2: 0},
        compiler_params=_cparams(1),
    )(pos_flat, h, xs)


def _expert_kernel(be_ref, nused_ref, xs_ref, wgu_ref, bgu_ref, wdn_ref, bdn_ref, ys_ref,
                   wgu_scr, wdn_scr):
    i = pl.program_id(0)
    prev_e = be_ref[jnp.maximum(i - 1, 0)]
    new_expert = jnp.logical_or(i == 0, be_ref[i] != prev_e)
    active = i < nused_ref[0]

    @pl.when(jnp.logical_and(active, new_expert))
    def _():
        wgu_scr[...] = wgu_ref[0].astype(BF16)
        wdn_scr[...] = wdn_ref[0].astype(BF16)

    @pl.when(active)
    def _():
        de = wdn_scr.shape[0]
        gu = jnp.dot(xs_ref[...].astype(BF16), wgu_scr[...],
                     preferred_element_type=F32) + bgu_ref[0]
        g = jnp.minimum(gu[:, :de], SWIGLU_LIMIT)
        u = jnp.clip(gu[:, de:], -SWIGLU_LIMIT, SWIGLU_LIMIT)
        act = (u + 1.0) * g * jax.nn.sigmoid(SWIGLU_ALPHA * g)
        ys_ref[...] = jnp.dot(act.astype(BF16), wdn_scr[...],
                              preferred_element_type=F32) + bdn_ref[0]

    @pl.when(jnp.logical_not(active))
    def _():
        ys_ref[...] = jnp.zeros_like(ys_ref)


def _experts(block_e, n_used, xs, w_gu, b_gu, w_dn, b_dn, rows):
    n_rows, d = xs.shape
    n_e, _, two_de = w_gu.shape
    de = two_de // 2
    n_blocks = n_rows // rows
    row_map = lambda i, be, nu: (jnp.minimum(i, nu[0] - 1), 0)
    e_map = lambda i, be, nu: (be[jnp.minimum(i, nu[0] - 1)], 0, 0)
    return pl.pallas_call(
        _expert_kernel,
        grid_spec=pltpu.PrefetchScalarGridSpec(
            num_scalar_prefetch=2,
            grid=(n_blocks,),
            in_specs=[pl.BlockSpec((rows, d), row_map),
                      pl.BlockSpec((1, d, two_de), e_map),
                      pl.BlockSpec((1, 1, two_de), e_map),
                      pl.BlockSpec((1, de, d), e_map),
                      pl.BlockSpec((1, 1, d), e_map)],
            out_specs=pl.BlockSpec((rows, d), lambda i, be, nu: (i, 0)),
            scratch_shapes=[pltpu.VMEM((d, two_de), BF16), pltpu.VMEM((de, d), BF16)]),
        out_shape=jax.ShapeDtypeStruct((n_rows, d), F32),
        compiler_params=_cparams(1, VMEM_LIMIT_BYTES),
    )(block_e, n_used, xs, w_gu, b_gu.reshape(n_e, 1, two_de), w_dn, b_dn.reshape(n_e, 1, d))


def _combine_kernel(pos_ref, ys_ref, gates_ref, x_ref, gate2_ref, o_ref, buf, sem):
    tm = x_ref.shape[1]
    n_i = pl.num_programs(1)
    g = pl.program_id(0) * n_i + pl.program_id(1)
    last = pl.num_programs(0) * n_i - 1
    slot = g % 2

    def gather(tile, to_slot):
        base = tile * (tm * TOP_K)
        for t in range(tm):
            for k in range(TOP_K):
                _row_copy(ys_ref, pos_ref[base + t * TOP_K + k], buf.at[to_slot, k], t,
                          sem.at[to_slot]).start(priority=k % 2)

    def drain(from_slot):
        for _ in range(tm * TOP_K):
            _row_copy(ys_ref, 0, buf.at[from_slot, 0], 0, sem.at[from_slot]).wait()

    @pl.when(g == 0)
    def _():
        gather(0, 0)

    gather(jnp.minimum(g + 1, last), 1 - slot)
    drain(slot)
    gates = gates_ref[0]
    ff = gates[:, 0:1] * buf[slot, 0]
    for k in range(1, TOP_K):
        ff = ff + gates[:, k:k + 1] * buf[slot, k]
    o_ref[0] = x_ref[0] + gate2_ref[0] * ff

    @pl.when(g == last)
    def _():
        drain(1 - slot)


def _combine(pos_flat, ys, gates_tk, x, gate2, tm):
    b, s, d = x.shape
    r = gate2.shape[1]
    return pl.pallas_call(
        _combine_kernel,
        grid_spec=pltpu.PrefetchScalarGridSpec(
            num_scalar_prefetch=1,
            grid=(b, s // tm),
            in_specs=[pl.BlockSpec(memory_space=pl.ANY),
                      pl.BlockSpec((1, tm, TOP_K), lambda bi, i, pos: (bi, i, 0)),
                      pl.BlockSpec((1, tm, d), lambda bi, i, pos: (bi, i, 0)),
                      pl.BlockSpec((1, r, d), lambda bi, i, pos: (bi, 0, 0))],
            out_specs=pl.BlockSpec((1, tm, d), lambda bi, i, pos: (bi, i, 0)),
            scratch_shapes=[pltpu.VMEM((2, TOP_K, tm, d), F32), pltpu.SemaphoreType.DMA((2,))]),
        out_shape=jax.ShapeDtypeStruct((b, s, d), F32),
        compiler_params=_cparams(2, VMEM_LIMIT_BYTES),
    )(pos_flat, ys, gates_tk, x, gate2)


def _route(ids, ranks, counts, rows, n_blocks):
    padded = (counts + rows - 1) // rows * rows
    pad_end = jnp.cumsum(padded)
    pad_start = pad_end - padded
    experts = jnp.arange(N_EXPERTS, dtype=jnp.int32)
    pos = jnp.sum(jnp.where(ids[..., None] == experts, pad_start, 0), axis=-1) + ranks
    first_row = jnp.arange(n_blocks, dtype=jnp.int32) * rows
    block_e = jnp.minimum(jnp.sum((pad_end[None, :] <= first_row[:, None]).astype(jnp.int32), axis=1),
                          N_EXPERTS - 1)
    n_used = jnp.maximum(pad_end[-1:] // rows, 1).astype(jnp.int32)
    return pos.astype(jnp.int32), block_e, n_used


def _tiles_to_tokens(a):
    n_tiles, k, tm = a.shape
    return jnp.transpose(a, (1, 0, 2)).reshape(k, n_tiles * tm)


def _split_hi_lo(w):
    hi = w.astype(BF16)
    return jnp.stack([hi, (w - hi.astype(F32)).astype(BF16)])


def _layer_weights(l, w_in, g_norm_mix, g_norm_ffn, a_q_norm, a_k_norm, b_ln_g, b_ln_b,
                   c_q_norm, c_k_norm, w_branch, w_out, w_router, b_router, widths):
    d = w_in.shape[1]
    aw, kvw = widths[0], widths[6]
    splits = np.cumsum(widths)[:-1].tolist()
    w_split = [w.astype(BF16) for w in jnp.split(w_in[l], splits, axis=-1)]
    head = np.arange(aw) // HEAD_DIM
    bd = jnp.asarray(head[:, None] == head[None, :], BF16)
    tile = lambda v, w: jnp.tile(v, w // HEAD_DIM).reshape(1, w)
    return dict(
        w_split=w_split, bd=bd,
        g_mix=g_norm_mix[l].reshape(1, d), g_ffn=g_norm_ffn[l].reshape(1, d),
        aqn=tile(a_q_norm[l], aw), akn=tile(a_k_norm[l], aw),
        cqn=tile(c_q_norm[l], aw), ckn=tile(c_k_norm[l], kvw),
        lng=b_ln_g[l].reshape(1, aw), lnb=b_ln_b[l].reshape(1, aw),
        w_br=w_branch[l].astype(BF16), w_out=w_out[l].astype(BF16),
        w_rt=_split_hi_lo(jnp.transpose(w_router[l])), b_r=b_router[l].reshape(N_EXPERTS, 1))


def kernel(x_prompt, x_sample, cache_a_k, cache_a_v, cache_c_k, cache_c_v, c_prompt, c_sample,
           w_ada, b_ada, g_norm_mix, g_norm_ffn, w_in, a_q_norm, a_k_norm, a_rel_bias,
           b_ln_g, b_ln_b, b_w_s, b_b_s, c_q_norm, c_k_norm, c_sinks, t5_bias,
           w_branch, w_out, w_router, b_router, w_gate_up, b_gate_up, w_down, b_down):
    bp, sp, d = x_prompt.shape
    bs, ss, _ = x_sample.shape
    depth = w_in.shape[0]
    ts = bs * ss
    tp = bp * sp
    a_heads, c_kv_heads = cache_a_k.shape[3], cache_c_k.shape[3]
    aw = a_heads * HEAD_DIM
    kvw = c_kv_heads * HEAD_DIM
    widths = [aw, aw, aw, aw, aw, aw, kvw, kvw, N_BRANCH * d]
    ra, rc = cache_a_k.shape[2], cache_c_k.shape[2]
    a_band, c_band = A_PREV_CHUNKS * CHUNK, C_PREV_CHUNKS * CHUNK
    proj_tile, merge_tile = min(PROJ_TILE, sp), min(MERGE_TILE, sp)
    a_tile, c_tile = min(A_Q_TILE, sp), min(C_Q_TILE, sp)
    moe_tile = min(MOE_TOKEN_TILE, tp)

    c_rows = jnp.concatenate([c_prompt, c_sample,
                              jnp.zeros((ADA_ROWS - bp - bs, d), F32)], axis=0)
    mod_all = _ada_modulation(c_rows, w_ada, b_ada).reshape(depth, ADA_ROWS, 6, d)

    t5_p = _c_bias_table(t5_bias, c_band, CHUNK, c_band + CHUNK)
    t5_s = _c_bias_table(t5_bias, rc, ss, rc + ss)
    bias_c_p = _band_bias(t5_p, c_tile, C_PREV_CHUNKS)

    n_assign = (tp + ts) * TOP_K
    n_blocks = -(-(n_assign + N_EXPERTS * (MOE_ROWS - 1)) // MOE_ROWS)
    n_rows = n_blocks * MOE_ROWS

    mask = (np.arange(B_CHUNK)[None, :] // CHUNK <= np.arange(B_CHUNK)[:, None] // CHUNK)
    mask = jnp.asarray(mask, F32)

    xp = x_prompt
    xs = x_sample.reshape(1, ts, d)
    st_p, st_s = [], []
    for l in range(depth):
        lw = _layer_weights(l, w_in, g_norm_mix, g_norm_ffn, a_q_norm, a_k_norm, b_ln_g, b_ln_b,
                            c_q_norm, c_k_norm, w_branch, w_out, w_router, b_router, widths)
        mod_p = mod_all[l, :bp]
        mod_s = jnp.repeat(mod_all[l, bp:bp + bs], ss, axis=0)
        mp = [mod_p[:, i].reshape(bp, 1, d) for i in range(6)]
        ms = [mod_s[:, i].reshape(1, ts, d) for i in range(6)]

        ws_m = b_w_s[l] * mask
        sgu_w_p = ws_m.astype(BF16)
        sgu_b_p = jnp.repeat(jnp.transpose(b_b_s[l]), aw // B_GROUPS, axis=1)
        eye = jnp.eye(B_CHUNK // ss, dtype=F32)
        sgu_w_s = jnp.stack([jnp.kron(eye, ws_m[g, :ss, :ss]) for g in range(B_GROUPS)]).astype(BF16)
        sgu_b_s = jnp.tile(sgu_b_p[:ss], (B_CHUNK // ss, 1))

        bias_a_p = _band_bias(_a_bias_table(a_rel_bias[l], a_band, CHUNK, a_band + CHUNK),
                              a_tile, A_PREV_CHUNKS)
        bias_a_s = _a_bias_table(a_rel_bias[l], ra, ss, ra + ss)
        sinks = c_sinks[l].astype(F32)

        aq, ak, av, bu, bv, cq, ck, cv, g = _project(xp, mp[0], mp[1], lw, proj_tile)
        oa = _band_attention(aq, ak, av, bias_a_p, None, a_band, a_tile)
        oc = _band_attention(cq, ck, cv, bias_c_p, sinks, c_band, c_tile)
        na, nc = min(a_band, sp), min(c_band, sp)
        st_p.append((ak[:, sp - na:].reshape(bp, na, a_heads, HEAD_DIM),
                     av[:, sp - na:].reshape(bp, na, a_heads, HEAD_DIM),
                     ck[:, sp - nc:].reshape(bp, nc, c_kv_heads, HEAD_DIM),
                     cv[:, sp - nc:].reshape(bp, nc, c_kv_heads, HEAD_DIM)))
        zero_counts = jnp.zeros((N_EXPERTS, 128), F32)
        xp_mid, h2_p, ids_p, gates_p, rank_p, counts_p = _merge(
            oa, oc, bu, bv, g, xp, mp[2], mp[3], mp[4], lw, sgu_w_p, sgu_b_p, zero_counts,
            merge_tile)

        aq, ak, av, bu, bv, cq, ck, cv, g = _project(xs, ms[0], ms[1], lw, ts)
        per_seq = lambda a: a.reshape(bs, ss, a.shape[-1])
        oa = _band_attention(per_seq(aq), per_seq(ak), per_seq(av), bias_a_s, None, ra, ss,
                             prev=(cache_a_k[l].reshape(bs, ra, aw), cache_a_v[l].reshape(bs, ra, aw)))
        oc = _band_attention(per_seq(cq), per_seq(ck), per_seq(cv), t5_s, sinks, rc, ss,
                             prev=(cache_c_k[l].reshape(bs, rc, kvw), cache_c_v[l].reshape(bs, rc, kvw)))
        st_s.append((ak.reshape(bs, ss, a_heads, HEAD_DIM), av.reshape(bs, ss, a_heads, HEAD_DIM),
                     ck.reshape(bs, ss, c_kv_heads, HEAD_DIM), cv.reshape(bs, ss, c_kv_heads, HEAD_DIM),
                     bv.reshape(bs, ss, aw)))
        xs_mid, h2_s, ids_s, gates_s, rank_s, counts = _merge(
            oa.reshape(1, ts, aw), oc.reshape(1, ts, aw), bu, bv, g, xs, ms[2], ms[3], ms[4], lw,
            sgu_w_s, sgu_b_s, counts_p, ts)

        ids = jnp.concatenate([_tiles_to_tokens(ids_p), _tiles_to_tokens(ids_s)], axis=1)
        ranks = jnp.concatenate([_tiles_to_tokens(rank_p), _tiles_to_tokens(rank_s)], axis=1)
        pos, block_e, n_used = _route(ids, ranks, counts[:, 0].astype(jnp.int32), MOE_ROWS, n_blocks)
        pos_tk = jnp.transpose(pos)
        pos_p = pos_tk[:tp].reshape(-1)
        pos_s = pos_tk[tp:].reshape(-1)
        rows_in = jnp.zeros((n_rows, d), F32)
        rows_in = _dispatch(pos_p, h2_p.reshape(tp, d), rows_in, moe_tile)
        rows_in = _dispatch(pos_s, h2_s.reshape(ts, d), rows_in, ts)
        n_e = w_gate_up.shape[1]
        rows_out = _experts(block_e + l * n_e, n_used, rows_in,
                            w_gate_up.reshape((depth * n_e,) + w_gate_up.shape[2:]),
                            b_gate_up.reshape(depth * n_e, -1),
                            w_down.reshape((depth * n_e,) + w_down.shape[2:]),
                            b_down.reshape(depth * n_e, -1), MOE_ROWS)
        gates_p_tk = jnp.transpose(_tiles_to_tokens(gates_p)).reshape(bp, sp, TOP_K)
        gates_s_tk = jnp.transpose(_tiles_to_tokens(gates_s)).reshape(1, ts, TOP_K)
        xp = _combine(pos_p, rows_out, gates_p_tk, xp_mid, mp[5], moe_tile)
        xs = _combine(pos_s, rows_out, gates_s_tk, xs_mid, ms[5], ts)

    stack = lambda rows, i: jnp.stack([r[i] for r in rows], axis=0)
    return (xp, xs.reshape(bs, ss, d),
            stack(st_p, 0), stack(st_p, 1), stack(st_p, 2), stack(st_p, 3),
            stack(st_s, 0), stack(st_s, 1), stack(st_s, 2), stack(st_s, 3), stack(st_s, 4))
```

```python
import functools
import math

import jax
import jax.numpy as jnp
import numpy as np
from jax import lax
from jax.experimental import pallas as pl
from jax.experimental.pallas import tpu as pltpu

F32 = jnp.float32
BF16 = jnp.bfloat16

CHUNK = 64
HEAD_DIM = 64
A_PREV_CHUNKS = 8
A_REL_CLIP = 128
B_CHUNK = 128
B_GROUPS = 4
C_PREV_CHUNKS = 2
T5_BUCKETS = 32
T5_MAX_DIST = 128
N_BRANCH = 3
N_EXPERTS = 32
TOP_K = 4
SWIGLU_ALPHA = 1.702
SWIGLU_LIMIT = 7.0
RMS_EPS = 1e-6
LN_EPS = 1e-5
NEG_INF = -1e30
F32_LOWEST = float(np.finfo(np.float32).min)

VMEM_LIMIT_BYTES = 56 * 1024 * 1024
ADA_ROWS = 16
PROJ_TILE = 512
MERGE_TILE = 256
A_Q_TILE = 256
C_Q_TILE = 128
MOE_ROWS = 512
MOE_TOKEN_TILE = 256
ZERO_ROWS = 8


def _cparams(n_axes, vmem=None):
    return pltpu.CompilerParams(dimension_semantics=("arbitrary",) * n_axes,
                                vmem_limit_bytes=vmem)


def _const_spec(shape):
    nd = len(shape)
    return pl.BlockSpec(shape, lambda *_: (0,) * nd, pipeline_mode=pl.Buffered(1))


def _ada_kernel(c_ref, w_ref, b_ref, o_ref):
    c = c_ref[...]
    s = c * jax.nn.sigmoid(c)
    o_ref[0] = jnp.dot(s, w_ref[0], preferred_element_type=F32,
                       precision=lax.Precision.HIGHEST) + b_ref[0]


def _ada_modulation(c_rows, w_ada, b_ada):
    depth, d, _ = w_ada.shape
    return pl.pallas_call(
        _ada_kernel,
        grid=(depth, 6),
        in_specs=[pl.BlockSpec((ADA_ROWS, d), lambda l, j: (0, 0)),
                  pl.BlockSpec((1, d, d), lambda l, j: (l, 0, j)),
                  pl.BlockSpec((1, 1, d), lambda l, j: (l, 0, j))],
        out_specs=pl.BlockSpec((1, ADA_ROWS, d), lambda l, j: (l, 0, j)),
        out_shape=jax.ShapeDtypeStruct((depth, ADA_ROWS, 6 * d), F32),
        compiler_params=_cparams(2),
    )(c_rows, w_ada, b_ada.reshape(depth, 1, 6 * d))


def _row_rms(x, gain):
    return x * lax.rsqrt(jnp.mean(x * x, axis=-1, keepdims=True) + RMS_EPS) * gain


def _head_rms(t, gain, bd):
    ss = jnp.dot((t * t).astype(BF16), bd, preferred_element_type=F32)
    return t * lax.rsqrt(ss * (1.0 / HEAD_DIM) + RMS_EPS) * gain


def _proj_kernel(x_ref, shift_ref, scale_ref, gmix_ref,
                 w_aq, w_ak, w_av, w_bu, w_bv, w_cq, w_ck, w_cv, w_g,
                 bd_ref, aqn, akn, cqn, ckn, lng, lnb,
                 aq_o, ak_o, av_o, bu_o, bv_o, cq_o, ck_o, cv_o, g_o):
    x = x_ref[0]
    h = _row_rms(x, gmix_ref[...]) * (1.0 + scale_ref[0]) + shift_ref[0]
    hb = h.astype(BF16)

    def mm(w_ref):
        return jnp.dot(hb, w_ref[...], preferred_element_type=F32)

    bd = bd_ref[...]
    q_scale = HEAD_DIM ** -0.5
    aq_o[0] = (_head_rms(mm(w_aq), aqn[...], bd) * q_scale).astype(BF16)
    ak_o[0] = _head_rms(mm(w_ak), akn[...], bd)
    av_o[0] = mm(w_av)
    bu_o[0] = jax.nn.gelu(mm(w_bu)).astype(BF16)
    bv = jax.nn.gelu(mm(w_bv))
    mu = jnp.mean(bv, axis=-1, keepdims=True)
    var = jnp.mean(jnp.square(bv - mu), axis=-1, keepdims=True)
    bv_o[0] = (bv - mu) * lax.rsqrt(var + LN_EPS) * lng[...] + lnb[...]
    cq_o[0] = (_head_rms(mm(w_cq), cqn[...], bd) * q_scale).astype(BF16)
    kvw = ck_o.shape[-1]
    ck_o[0] = _head_rms(mm(w_ck), ckn[...], bd[:kvw, :kvw])
    cv_o[0] = mm(w_cv)
    d = x.shape[-1]
    for r in range(N_BRANCH):
        g_o[0, :, r * d:(r + 1) * d] = jax.nn.sigmoid(
            jnp.dot(hb, w_g[:, r * d:(r + 1) * d], preferred_element_type=F32)).astype(BF16)


def _project(x, shift, scale, lw, tm):
    b, s, d = x.shape
    r = shift.shape[1]
    widths = [w.shape[1] for w in lw["w_split"]]
    aw, kvw = widths[0], widths[6]
    tok = lambda w: pl.BlockSpec((1, tm, w), lambda bi, i: (bi, i, 0))
    mod = pl.BlockSpec((1, r, d), lambda bi, i: (bi, 0, 0))
    in_specs = ([tok(d), mod, mod, _const_spec((1, d))]
                + [_const_spec(w.shape) for w in lw["w_split"]]
                + [_const_spec(lw["bd"].shape)]
                + [_const_spec((1, aw)), _const_spec((1, aw)), _const_spec((1, aw)),
                   _const_spec((1, kvw)), _const_spec((1, aw)), _const_spec((1, aw))])
    out_dt = [BF16, F32, F32, BF16, F32, BF16, F32, F32, BF16]
    return pl.pallas_call(
        _proj_kernel,
        grid=(b, s // tm),
        in_specs=in_specs,
        out_specs=[tok(w) for w in widths],
        out_shape=[jax.ShapeDtypeStruct((b, s, w), dt) for w, dt in zip(widths, out_dt)],
        compiler_params=_cparams(2, VMEM_LIMIT_BYTES),
    )(x, shift, scale, lw["g_mix"], *lw["w_split"], lw["bd"],
      lw["aqn"], lw["akn"], lw["cqn"], lw["ckn"], lw["lng"], lw["lnb"])


def _attn_kernel(*refs, n_q_heads, n_kv_heads, has_sink, carry):
    refs = list(refs)
    q_ref, k_ref, v_ref = refs[:3]
    pos = 3
    if not carry:
        pk_ref, pv_ref = refs[pos:pos + 2]
        pos += 2
    bias_ref = refs[pos]
    pos += 1
    if has_sink:
        sink_ref = refs[pos]
        pos += 1
    o_ref = refs[pos]
    pos += 1
    if carry:
        pk_scr, pv_scr = refs[pos:pos + 2]

    j = pl.program_id(1)
    kc = k_ref[0].astype(BF16)
    vc = v_ref[0].astype(BF16)
    tq = kc.shape[0]
    if carry:
        @pl.when(j == 0)
        def _():
            pk_scr[...] = jnp.zeros_like(pk_scr)
            pv_scr[...] = jnp.zeros_like(pv_scr)
        kp, vp = pk_scr[...], pv_scr[...]
    else:
        kp, vp = pk_ref[0].astype(BF16), pv_ref[0].astype(BF16)
    n_prev = kp.shape[0]
    kw = jnp.concatenate([kp, kc], axis=0)
    vw = jnp.concatenate([vp, vc], axis=0)
    width = n_prev + tq
    if carry:
        before_start = lax.broadcasted_iota(jnp.int32, (tq, width), 1) < (n_prev - j * tq)
    q = q_ref[0]
    group = n_q_heads // n_kv_heads
    outs = []
    for h in range(n_q_heads):
        kh = h // group
        qh = q[:, h * HEAD_DIM:(h + 1) * HEAD_DIM]
        s = lax.dot_general(qh, kw[:, kh * HEAD_DIM:(kh + 1) * HEAD_DIM],
                            (((1,), (1,)), ((), ())), preferred_element_type=F32) + bias_ref[h]
        if carry:
            s = jnp.where(before_start, NEG_INF, s)
        m = jnp.max(s, axis=-1, keepdims=True)
        if has_sink:
            sink = sink_ref[h]
            m = jnp.maximum(m, sink)
        e = jnp.exp(s - m)
        denom = jnp.sum(e, axis=-1, keepdims=True)
        if has_sink:
            denom = denom + jnp.exp(sink - m)
        o = jnp.dot(e.astype(BF16), vw[:, kh * HEAD_DIM:(kh + 1) * HEAD_DIM],
                    preferred_element_type=F32)
        outs.append(o / denom)
    o_ref[0] = jnp.concatenate(outs, axis=-1).astype(BF16)
    if carry:
        pk_scr[...] = kw[tq:, :]
        pv_scr[...] = vw[tq:, :]


def _band_attention(q, k, v, bias, sinks, n_prev_rows, tq, prev=None):
    b, s, qw = q.shape
    kvw = k.shape[-1]
    hq, kvh = qw // HEAD_DIM, kvw // HEAD_DIM
    carry = prev is None
    width = n_prev_rows + tq
    tok = lambda w: pl.BlockSpec((1, tq, w), lambda bi, i: (bi, i, 0))
    in_specs = [tok(qw), tok(kvw), tok(kvw)]
    args = [q, k, v]
    if not carry:
        pspec = pl.BlockSpec((1, n_prev_rows, kvw), lambda bi, i: (bi, 0, 0))
        in_specs += [pspec, pspec]
        args += list(prev)
    in_specs.append(_const_spec((hq, tq, width)))
    args.append(bias)
    if sinks is not None:
        in_specs.append(pl.BlockSpec(memory_space=pltpu.SMEM))
        args.append(sinks)
    scratch = [pltpu.VMEM((n_prev_rows, kvw), BF16)] * 2 if carry else []
    return pl.pallas_call(
        functools.partial(_attn_kernel, n_q_heads=hq, n_kv_heads=kvh,
                          has_sink=sinks is not None, carry=carry),
        grid=(b, s // tq),
        in_specs=in_specs,
        out_specs=tok(qw),
        out_shape=jax.ShapeDtypeStruct((b, s, qw), BF16),
        scratch_shapes=scratch,
        compiler_params=_cparams(2, VMEM_LIMIT_BYTES),
    )(*args)


def _band_bias(base, tq, n_prev_chunks):
    n_qc = tq // CHUNK
    blocks = [jnp.pad(base, ((0, 0), (0, 0), (jq * CHUNK, (n_qc - 1 - jq) * CHUNK)),
                      constant_values=NEG_INF) for jq in range(n_qc)]
    return jnp.concatenate(blocks, axis=1)


def _toeplitz(vec, n_q, n_k):
    return jnp.stack([vec[:, n_q - 1 - i:n_q - 1 - i + n_k] for i in range(n_q)], axis=1)


def _a_bias_table(a_rel, q0, n_q, n_k):
    t = np.arange(n_q + n_k - 1)
    idx = np.clip(q0 + n_q - 1 - t, -A_REL_CLIP, A_REL_CLIP) + A_REL_CLIP
    return _toeplitz(a_rel[:, idx].astype(F32), n_q, n_k)


def _t5_bucket(rel):
    n = T5_BUCKETS // 2
    max_exact = n // 2
    ret = jnp.where(rel > 0, n, 0)
    a = jnp.abs(rel)
    large = max_exact + (jnp.log(jnp.maximum(a, 1).astype(F32) / max_exact)
                         / math.log(T5_MAX_DIST / max_exact) * (n - max_exact)).astype(jnp.int32)
    large = jnp.minimum(large, n - 1)
    return ret + jnp.where(a < max_exact, a, large)


def _c_bias_table(t5_table, q0, n_q, n_k):
    rel = jnp.arange(n_q + n_k - 1, dtype=jnp.int32) - (n_q - 1) - q0
    vec = jnp.transpose(t5_table[_t5_bucket(rel)]).astype(F32)
    return _toeplitz(vec, n_q, n_k)


def _merge_kernel(oa_ref, oc_ref, bu_ref, bv_ref, g_ref, x_ref, gate1_ref, shift2_ref, scale2_ref,
                  gffn_ref, wbr_ref, wout_ref, ws_ref, bsb_ref, wrt_ref, br_ref, tri_ref, cnt_ref,
                  xnew_o, h2_o, ids_o, gates_o, rank_o, cnt_o, base_scr):
    first = jnp.logical_and(pl.program_id(0) == 0, pl.program_id(1) == 0)

    @pl.when(first)
    def _():
        base_scr[...] = cnt_ref[...]

    tm, d = x_ref.shape[1], x_ref.shape[2]
    bw = bv_ref.shape[2]
    gch = bw // B_GROUPS
    bvb = bv_ref[0].astype(BF16)
    rows = []
    for r in range(tm // B_CHUNK):
        cols = [jnp.dot(ws_ref[gi], bvb[r * B_CHUNK:(r + 1) * B_CHUNK, gi * gch:(gi + 1) * gch],
                        preferred_element_type=F32) for gi in range(B_GROUPS)]
        rows.append(jnp.concatenate(cols, axis=-1) + bsb_ref[...])
    mixed = jnp.concatenate(rows, axis=0) if len(rows) > 1 else rows[0]
    ob = (bu_ref[0].astype(F32) * mixed).astype(BF16)

    branches = (oa_ref[0], ob, oc_ref[0])
    y = None
    for r in range(N_BRANCH):
        br = jnp.dot(branches[r], wbr_ref[r], preferred_element_type=F32)
        term = g_ref[0, :, r * d:(r + 1) * d].astype(F32) * br
        y = term if y is None else y + term
    out = jnp.dot(y.astype(BF16), wout_ref[...], preferred_element_type=F32)
    xn = x_ref[0] + gate1_ref[0] * out
    xnew_o[0] = xn
    h2 = _row_rms(xn, gffn_ref[...]) * (1.0 + scale2_ref[0]) + shift2_ref[0]
    h2_o[0] = h2

    nt = (((1,), (1,)), ((), ()))
    h_hi = h2.astype(BF16)
    h_lo = (h2 - h_hi.astype(F32)).astype(BF16)
    logits = (lax.dot_general(wrt_ref[0], h_hi, nt, preferred_element_type=F32)
              + lax.dot_general(wrt_ref[0], h_lo, nt, preferred_element_type=F32)
              + lax.dot_general(wrt_ref[1], h_hi, nt, preferred_element_type=F32)) + br_ref[...]
    n_exp = logits.shape[0]
    eidx = lax.broadcasted_iota(jnp.int32, (n_exp, tm), 0).astype(F32)
    vals, ids = [], []
    rest = logits
    for _ in range(TOP_K):
        m = jnp.max(rest, axis=0, keepdims=True)
        sel = jnp.min(jnp.where(rest == m, eidx, float(n_exp)), axis=0, keepdims=True)
        vals.append(m)
        ids.append(sel)
        rest = jnp.where(eidx == sel, F32_LOWEST, rest)
    exps = [jnp.exp(v - vals[0]) for v in vals]
    tot = exps[0] + exps[1] + exps[2] + exps[3]
    gates_o[0] = jnp.concatenate([e / tot for e in exps], axis=0)
    ids_o[0] = jnp.concatenate(ids, axis=0).astype(jnp.int32)

    base = base_scr[...]
    ranks = []
    for k in range(TOP_K):
        hit = eidx == ids[k]
        onehot = jnp.where(hit, 1.0, 0.0)
        before = jnp.dot(onehot.astype(BF16), tri_ref[...], preferred_element_type=F32)
        ranks.append(jnp.sum(jnp.where(hit, base[:, 0:1] + before, 0.0), axis=0, keepdims=True))
        base = base + jnp.sum(onehot, axis=1, keepdims=True)
    rank_o[0] = jnp.concatenate(ranks, axis=0).astype(jnp.int32)
    base_scr[...] = base
    cnt_o[...] = base


def _merge(oa, oc, bu, bv, g, x, gate1, shift2, scale2, lw, sgu_w, sgu_b, counts, tm):
    b, s, d = x.shape
    r = gate1.shape[1]
    bw = bu.shape[-1]
    n_tiles = b * (s // tm)
    tok = lambda w: pl.BlockSpec((1, tm, w), lambda bi, i: (bi, i, 0))
    mod = pl.BlockSpec((1, r, d), lambda bi, i: (bi, 0, 0))
    per_tile = pl.BlockSpec((1, TOP_K, tm), lambda bi, i: (bi * (s // tm) + i, 0, 0))
    tri = jnp.asarray(np.triu(np.ones((tm, tm), np.float32), k=1), BF16)
    in_specs = [tok(bw), tok(bw), tok(bw), tok(bw), tok(N_BRANCH * d), tok(d), mod, mod, mod,
                _const_spec((1, d)), _const_spec(lw["w_br"].shape), _const_spec((d, d)),
                _const_spec(sgu_w.shape), _const_spec(sgu_b.shape),
                _const_spec((2, N_EXPERTS, d)), _const_spec((N_EXPERTS, 1)),
                _const_spec((tm, tm)), _const_spec((N_EXPERTS, 128))]
    out_shape = [jax.ShapeDtypeStruct((b, s, d), F32), jax.ShapeDtypeStruct((b, s, d), F32),
                 jax.ShapeDtypeStruct((n_tiles, TOP_K, tm), jnp.int32),
                 jax.ShapeDtypeStruct((n_tiles, TOP_K, tm), F32),
                 jax.ShapeDtypeStruct((n_tiles, TOP_K, tm), jnp.int32),
                 jax.ShapeDtypeStruct((N_EXPERTS, 128), F32)]
    out_specs = [tok(d), tok(d), per_tile, per_tile, per_tile, _const_spec((N_EXPERTS, 128))]
    return pl.pallas_call(
        _merge_kernel,
        grid=(b, s // tm),
        in_specs=in_specs,
        out_specs=out_specs,
        out_shape=out_shape,
        scratch_shapes=[pltpu.VMEM((N_EXPERTS, 128), F32)],
        compiler_params=_cparams(2, VMEM_LIMIT_BYTES),
    )(oa, oc, bu, bv, g, x, gate1, shift2, scale2, lw["g_ffn"], lw["w_br"], lw["w_out"],
      sgu_w, sgu_b, lw["w_rt"], lw["b_r"], tri, counts)


def _row_copy(src_ref, src_row, dst_ref, dst_row, sem):
    return pltpu.make_async_copy(src_ref.at[pl.ds(src_row, 1)], dst_ref.at[pl.ds(dst_row, 1)], sem)


def _scatter_rows(pos_ref, base, h_ref, xs_ref, sem):
    tm = h_ref.shape[0]
    for t in range(tm):
        for k in range(TOP_K):
            _row_copy(h_ref, t, xs_ref, pos_ref[base + t * TOP_K + k], sem).start(priority=k % 2)
    for _ in range(tm * TOP_K):
        _row_copy(h_ref, 0, xs_ref, 0, sem).wait()


def _dispatch_kernel(posp_ref, poss_ref, zstart_ref, zchunks_ref, hp_ref, hs_ref, xs_ref,
                     zero_scr, sem):
    i = pl.program_id(0)
    n_p = pl.num_programs(0) - 2
    zrows = zero_scr.shape[0]

    def zero_copy(row):
        return pltpu.make_async_copy(zero_scr, xs_ref.at[pl.ds(row, zrows)], sem)

    @pl.when(i == 0)
    def _():
        zero_scr[...] = jnp.zeros_like(zero_scr)

        def per_range(r, carry):
            start = zstart_ref[r]

            def issue(j, c):
                zero_copy(pl.multiple_of(start + j * zrows, zrows)).start()
                return c

            lax.fori_loop(0, zchunks_ref[r], issue, 0)
            return carry

        def drain_range(r, carry):
            def drain(j, c):
                zero_copy(0).wait()
                return c

            lax.fori_loop(0, zchunks_ref[r], drain, 0)
            return carry

        lax.fori_loop(0, zstart_ref.shape[0], per_range, 0)
        lax.fori_loop(0, zstart_ref.shape[0], drain_range, 0)

    @pl.when(jnp.logical_and(i >= 1, i <= n_p))
    def _():
        _scatter_rows(posp_ref, (i - 1) * (hp_ref.shape[0] * TOP_K), hp_ref, xs_ref, sem)

    @pl.when(i == n_p + 1)
    def _():
        _scatter_rows(poss_ref, 0, hs_ref, xs_ref, sem)


def _dispatch(pos_p, pos_s, zstart, zchunks, h_p, h_s, n_rows, tm):
    tp, d = h_p.shape
    ts = h_s.shape[0]
    n_p = tp // tm
    return pl.pallas_call(
        _dispatch_kernel,
        grid_spec=pltpu.PrefetchScalarGridSpec(
            num_scalar_prefetch=4,
            grid=(n_p + 2,),
            in_specs=[pl.BlockSpec((tm, d), lambda i, *_: (jnp.clip(i - 1, 0, n_p - 1), 0)),
                      pl.BlockSpec((ts, d), lambda i, *_: (0, 0))],
            out_specs=pl.BlockSpec(memory_space=pl.ANY),
            scratch_shapes=[pltpu.VMEM((ZERO_ROWS, d), F32), pltpu.SemaphoreType.DMA]),
        out_shape=jax.ShapeDtypeStruct((n_rows, d), F32),
        compiler_params=_cparams(1),
    )(pos_p, pos_s, zstart, zchunks, h_p, h_s)


def _expert_kernel(be_ref, nused_ref, xs_ref, wgu_ref, bgu_ref, wdn_ref, bdn_ref, ys_ref,
                   wgu_scr, wdn_scr):
    i = pl.program_id(0)
    prev_e = be_ref[jnp.maximum(i - 1, 0)]
    new_expert = jnp.logical_or(i == 0, be_ref[i] != prev_e)
    active = i < nused_ref[0]

    @pl.when(jnp.logical_and(active, new_expert))
    def _():
        wgu_scr[...] = wgu_ref[0].astype(BF16)
        wdn_scr[...] = wdn_ref[0].astype(BF16)

    @pl.when(active)
    def _():
        de = wdn_scr.shape[0]
        gu = jnp.dot(xs_ref[...].astype(BF16), wgu_scr[...],
                     preferred_element_type=F32) + bgu_ref[0]
        g = jnp.minimum(gu[:, :de], SWIGLU_LIMIT)
        u = jnp.clip(gu[:, de:], -SWIGLU_LIMIT, SWIGLU_LIMIT)
        act = (u + 1.0) * g * jax.nn.sigmoid(SWIGLU_ALPHA * g)
        ys_ref[...] = jnp.dot(act.astype(BF16), wdn_scr[...],
                              preferred_element_type=F32) + bdn_ref[0]

    @pl.when(jnp.logical_not(active))
    def _():
        ys_ref[...] = jnp.zeros_like(ys_ref)


def _experts(block_e, n_used, xs, w_gu, b_gu, w_dn, b_dn, rows):
    n_rows, d = xs.shape
    n_e, _, two_de = w_gu.shape
    de = two_de // 2
    n_blocks = n_rows // rows
    row_map = lambda i, be, nu: (jnp.minimum(i, nu[0] - 1), 0)
    e_map = lambda i, be, nu: (be[jnp.minimum(i, nu[0] - 1)], 0, 0)
    return pl.pallas_call(
        _expert_kernel,
        grid_spec=pltpu.PrefetchScalarGridSpec(
            num_scalar_prefetch=2,
            grid=(n_blocks,),
            in_specs=[pl.BlockSpec((rows, d), row_map),
                      pl.BlockSpec((1, d, two_de), e_map),
                      pl.BlockSpec((1, 1, two_de), e_map),
                      pl.BlockSpec((1, de, d), e_map),
                      pl.BlockSpec((1, 1, d), e_map)],
            out_specs=pl.BlockSpec((rows, d), lambda i, be, nu: (i, 0)),
            scratch_shapes=[pltpu.VMEM((d, two_de), BF16), pltpu.VMEM((de, d), BF16)]),
        out_shape=jax.ShapeDtypeStruct((n_rows, d), F32),
        compiler_params=_cparams(1, VMEM_LIMIT_BYTES),
    )(block_e, n_used, xs, w_gu, b_gu.reshape(n_e, 1, two_de), w_dn, b_dn.reshape(n_e, 1, d))


def _combine_kernel(pos_ref, ys_ref, gates_ref, x_ref, gate2_ref, o_ref, buf, sem):
    tm = x_ref.shape[1]
    n_i = pl.num_programs(1)
    g = pl.program_id(0) * n_i + pl.program_id(1)
    last = pl.num_programs(0) * n_i - 1
    slot = g % 2

    def gather(tile, to_slot):
        base = tile * (tm * TOP_K)
        for t in range(tm):
            for k in range(TOP_K):
                _row_copy(ys_ref, pos_ref[base + t * TOP_K + k], buf.at[to_slot, k], t,
                          sem.at[to_slot]).start(priority=k % 2)

    def drain(from_slot):
        for _ in range(tm * TOP_K):
            _row_copy(ys_ref, 0, buf.at[from_slot, 0], 0, sem.at[from_slot]).wait()

    @pl.when(g == 0)
    def _():
        gather(0, 0)

    gather(jnp.minimum(g + 1, last), 1 - slot)
    drain(slot)
    gates = gates_ref[0]
    ff = gates[:, 0:1] * buf[slot, 0]
    for k in range(1, TOP_K):
        ff = ff + gates[:, k:k + 1] * buf[slot, k]
    o_ref[0] = x_ref[0] + gate2_ref[0] * ff

    @pl.when(g == last)
    def _():
        drain(1 - slot)


def _combine(pos_flat, ys, gates_tk, x, gate2, tm):
    b, s, d = x.shape
    r = gate2.shape[1]
    return pl.pallas_call(
        _combine_kernel,
        grid_spec=pltpu.PrefetchScalarGridSpec(
            num_scalar_prefetch=1,
            grid=(b, s // tm),
            in_specs=[pl.BlockSpec(memory_space=pl.ANY),
                      pl.BlockSpec((1, tm, TOP_K), lambda bi, i, pos: (bi, i, 0)),
                      pl.BlockSpec((1, tm, d), lambda bi, i, pos: (bi, i, 0)),
                      pl.BlockSpec((1, r, d), lambda bi, i, pos: (bi, 0, 0))],
            out_specs=pl.BlockSpec((1, tm, d), lambda bi, i, pos: (bi, i, 0)),
            scratch_shapes=[pltpu.VMEM((2, TOP_K, tm, d), F32), pltpu.SemaphoreType.DMA((2,))]),
        out_shape=jax.ShapeDtypeStruct((b, s, d), F32),
        compiler_params=_cparams(2, VMEM_LIMIT_BYTES),
    )(pos_flat, ys, gates_tk, x, gate2)


def _route(ids, ranks, counts, rows, n_blocks):
    padded = (counts + rows - 1) // rows * rows
    pad_end = jnp.cumsum(padded)
    pad_start = pad_end - padded
    experts = jnp.arange(N_EXPERTS, dtype=jnp.int32)
    pos = jnp.sum(jnp.where(ids[..., None] == experts, pad_start, 0), axis=-1) + ranks
    first_row = jnp.arange(n_blocks, dtype=jnp.int32) * rows
    block_e = jnp.minimum(jnp.sum((pad_end[None, :] <= first_row[:, None]).astype(jnp.int32), axis=1),
                          N_EXPERTS - 1)
    n_used = jnp.maximum(pad_end[-1:] // rows, 1).astype(jnp.int32)
    group_end = (pad_start + counts) // ZERO_ROWS * ZERO_ROWS
    zstart = jnp.concatenate([group_end, pad_end[-1:]]).astype(jnp.int32)
    zstop = jnp.concatenate([pad_end, jnp.full((1,), n_blocks * rows, pad_end.dtype)])
    zchunks = ((zstop - zstart) // ZERO_ROWS).astype(jnp.int32)
    return pos.astype(jnp.int32), block_e, n_used, zstart, zchunks


def _tiles_to_tokens(a):
    n_tiles, k, tm = a.shape
    return jnp.transpose(a, (1, 0, 2)).reshape(k, n_tiles * tm)


def _split_hi_lo(w):
    hi = w.astype(BF16)
    return jnp.stack([hi, (w - hi.astype(F32)).astype(BF16)])


def _layer_weights(l, w_in, g_norm_mix, g_norm_ffn, a_q_norm, a_k_norm, b_ln_g, b_ln_b,
                   c_q_norm, c_k_norm, w_branch, w_out, w_router, b_router, widths):
    d = w_in.shape[1]
    aw, kvw = widths[0], widths[6]
    splits = np.cumsum(widths)[:-1].tolist()
    w_split = [w.astype(BF16) for w in jnp.split(w_in[l], splits, axis=-1)]
    head = np.arange(aw) // HEAD_DIM
    bd = jnp.asarray(head[:, None] == head[None, :], BF16)
    tile = lambda v, w: jnp.tile(v, w // HEAD_DIM).reshape(1, w)
    return dict(
        w_split=w_split, bd=bd,
        g_mix=g_norm_mix[l].reshape(1, d), g_ffn=g_norm_ffn[l].reshape(1, d),
        aqn=tile(a_q_norm[l], aw), akn=tile(a_k_norm[l], aw),
        cqn=tile(c_q_norm[l], aw), ckn=tile(c_k_norm[l], kvw),
        lng=b_ln_g[l].reshape(1, aw), lnb=b_ln_b[l].reshape(1, aw),
        w_br=w_branch[l].astype(BF16), w_out=w_out[l].astype(BF16),
        w_rt=_split_hi_lo(jnp.transpose(w_router[l])), b_r=b_router[l].reshape(N_EXPERTS, 1))


def kernel(x_prompt, x_sample, cache_a_k, cache_a_v, cache_c_k, cache_c_v, c_prompt, c_sample,
           w_ada, b_ada, g_norm_mix, g_norm_ffn, w_in, a_q_norm, a_k_norm, a_rel_bias,
           b_ln_g, b_ln_b, b_w_s, b_b_s, c_q_norm, c_k_norm, c_sinks, t5_bias,
           w_branch, w_out, w_router, b_router, w_gate_up, b_gate_up, w_down, b_down):
    bp, sp, d = x_prompt.shape
    bs, ss, _ = x_sample.shape
    depth = w_in.shape[0]
    ts = bs * ss
    tp = bp * sp
    a_heads, c_kv_heads = cache_a_k.shape[3], cache_c_k.shape[3]
    aw = a_heads * HEAD_DIM
    kvw = c_kv_heads * HEAD_DIM
    widths = [aw, aw, aw, aw, aw, aw, kvw, kvw, N_BRANCH * d]
    ra, rc = cache_a_k.shape[2], cache_c_k.shape[2]
    a_band, c_band = A_PREV_CHUNKS * CHUNK, C_PREV_CHUNKS * CHUNK
    proj_tile, merge_tile = min(PROJ_TILE, sp), min(MERGE_TILE, sp)
    a_tile, c_tile = min(A_Q_TILE, sp), min(C_Q_TILE, sp)
    moe_tile = min(MOE_TOKEN_TILE, tp)

    c_rows = jnp.concatenate([c_prompt, c_sample,
                              jnp.zeros((ADA_ROWS - bp - bs, d), F32)], axis=0)
    mod_all = _ada_modulation(c_rows, w_ada, b_ada).reshape(depth, ADA_ROWS, 6, d)

    t5_p = _c_bias_table(t5_bias, c_band, CHUNK, c_band + CHUNK)
    t5_s = _c_bias_table(t5_bias, rc, ss, rc + ss)
    bias_c_p = _band_bias(t5_p, c_tile, C_PREV_CHUNKS)

    n_assign = (tp + ts) * TOP_K
    n_blocks = -(-(n_assign + N_EXPERTS * (MOE_ROWS - 1)) // MOE_ROWS)
    n_rows = n_blocks * MOE_ROWS

    mask = (np.arange(B_CHUNK)[None, :] // CHUNK <= np.arange(B_CHUNK)[:, None] // CHUNK)
    mask = jnp.asarray(mask, F32)

    xp = x_prompt
    xs = x_sample.reshape(1, ts, d)
    st_p, st_s = [], []
    for l in range(depth):
        lw = _layer_weights(l, w_in, g_norm_mix, g_norm_ffn, a_q_norm, a_k_norm, b_ln_g, b_ln_b,
                            c_q_norm, c_k_norm, w_branch, w_out, w_router, b_router, widths)
        mod_p = mod_all[l, :bp]
        mod_s = jnp.repeat(mod_all[l, bp:bp + bs], ss, axis=0)
        mp = [mod_p[:, i].reshape(bp, 1, d) for i in range(6)]
        ms = [mod_s[:, i].reshape(1, ts, d) for i in range(6)]

        ws_m = b_w_s[l] * mask
        sgu_w_p = ws_m.astype(BF16)
        sgu_b_p = jnp.repeat(jnp.transpose(b_b_s[l]), aw // B_GROUPS, axis=1)
        eye = jnp.eye(B_CHUNK // ss, dtype=F32)
        sgu_w_s = jnp.stack([jnp.kron(eye, ws_m[g, :ss, :ss]) for g in range(B_GROUPS)]).astype(BF16)
        sgu_b_s = jnp.tile(sgu_b_p[:ss], (B_CHUNK // ss, 1))

        bias_a_p = _band_bias(_a_bias_table(a_rel_bias[l], a_band, CHUNK, a_band + CHUNK),
                              a_tile, A_PREV_CHUNKS)
        bias_a_s = _a_bias_table(a_rel_bias[l], ra, ss, ra + ss)
        sinks = c_sinks[l].astype(F32)

        aq, ak, av, bu, bv, cq, ck, cv, g = _project(xp, mp[0], mp[1], lw, proj_tile)
        oa = _band_attention(aq, ak, av, bias_a_p, None, a_band, a_tile)
        oc = _band_attention(cq, ck, cv, bias_c_p, sinks, c_band, c_tile)
        na, nc = min(a_band, sp), min(c_band, sp)
        st_p.append((ak[:, sp - na:].reshape(bp, na, a_heads, HEAD_DIM),
                     av[:, sp - na:].reshape(bp, na, a_heads, HEAD_DIM),
                     ck[:, sp - nc:].reshape(bp, nc, c_kv_heads, HEAD_DIM),
                     cv[:, sp - nc:].reshape(bp, nc, c_kv_heads, HEAD_DIM)))
        zero_counts = jnp.zeros((N_EXPERTS, 128), F32)
        xp_mid, h2_p, ids_p, gates_p, rank_p, counts_p = _merge(
            oa, oc, bu, bv, g, xp, mp[2], mp[3], mp[4], lw, sgu_w_p, sgu_b_p, zero_counts,
            merge_tile)

        aq, ak, av, bu, bv, cq, ck, cv, g = _project(xs, ms[0], ms[1], lw, ts)
        per_seq = lambda a: a.reshape(bs, ss, a.shape[-1])
        oa = _band_attention(per_seq(aq), per_seq(ak), per_seq(av), bias_a_s, None, ra, ss,
                             prev=(cache_a_k[l].reshape(bs, ra, aw), cache_a_v[l].reshape(bs, ra, aw)))
        oc = _band_attention(per_seq(cq), per_seq(ck), per_seq(cv), t5_s, sinks, rc, ss,
                             prev=(cache_c_k[l].reshape(bs, rc, kvw), cache_c_v[l].reshape(bs, rc, kvw)))
        st_s.append((ak.reshape(bs, ss, a_heads, HEAD_DIM), av.reshape(bs, ss, a_heads, HEAD_DIM),
                     ck.reshape(bs, ss, c_kv_heads, HEAD_DIM), cv.reshape(bs, ss, c_kv_heads, HEAD_DIM),
                     bv.reshape(bs, ss, aw)))
        xs_mid, h2_s, ids_s, gates_s, rank_s, counts = _merge(
            oa.reshape(1, ts, aw), oc.reshape(1, ts, aw), bu, bv, g, xs, ms[2], ms[3], ms[4], lw,
            sgu_w_s, sgu_b_s, counts_p, ts)

        ids = jnp.concatenate([_tiles_to_tokens(ids_p), _tiles_to_tokens(ids_s)], axis=1)
        ranks = jnp.concatenate([_tiles_to_tokens(rank_p), _tiles_to_tokens(rank_s)], axis=1)
        pos, block_e, n_used, zstart, zchunks = _route(
            ids, ranks, counts[:, 0].astype(jnp.int32), MOE_ROWS, n_blocks)
        pos_tk = jnp.transpose(pos)
        pos_p = pos_tk[:tp].reshape(-1)
        pos_s = pos_tk[tp:].reshape(-1)
        rows_in = _dispatch(pos_p, pos_s, zstart, zchunks, h2_p.reshape(tp, d), h2_s.reshape(ts, d),
                            n_rows, moe_tile)
        n_e = w_gate_up.shape[1]
        rows_out = _experts(block_e + l * n_e, n_used, rows_in,
                            w_gate_up.reshape((depth * n_e,) + w_gate_up.shape[2:]),
                            b_gate_up.reshape(depth * n_e, -1),
                            w_down.reshape((depth * n_e,) + w_down.shape[2:]),
                            b_down.reshape(depth * n_e, -1), MOE_ROWS)
        gates_p_tk = jnp.transpose(_tiles_to_tokens(gates_p)).reshape(bp, sp, TOP_K)
        gates_s_tk = jnp.transpose(_tiles_to_tokens(gates_s)).reshape(1, ts, TOP_K)
        xp = _combine(pos_p, rows_out, gates_p_tk, xp_mid, mp[5], moe_tile)
        xs = _combine(pos_s, rows_out, gates_s_tk, xs_mid, ms[5], ts)

    stack = lambda rows, i: jnp.stack([r[i] for r in rows], axis=0)
    return (xp, xs.reshape(bs, ss, d),
            stack(st_p, 0), stack(st_p, 1), stack(st_p, 2), stack(st_p, 3),
            stack(st_s, 0), stack(st_s, 1), stack(st_s, 2), stack(st_s, 3), stack(st_s, 4))
```

```python
import functools
import math

import jax
import jax.numpy as jnp
import numpy as np
from jax import lax
from jax.experimental import pallas as pl
from jax.experimental.pallas import tpu as pltpu

F32 = jnp.float32
BF16 = jnp.bfloat16

CHUNK = 64
HEAD_DIM = 64
A_PREV_CHUNKS = 8
A_REL_CLIP = 128
B_CHUNK = 128
B_GROUPS = 4
C_PREV_CHUNKS = 2
T5_BUCKETS = 32
T5_MAX_DIST = 128
N_BRANCH = 3
N_EXPERTS = 32
TOP_K = 4
SWIGLU_ALPHA = 1.702
SWIGLU_LIMIT = 7.0
RMS_EPS = 1e-6
LN_EPS = 1e-5
NEG_INF = -1e30
F32_LOWEST = float(np.finfo(np.float32).min)

VMEM_LIMIT_BYTES = 56 * 1024 * 1024
ADA_ROWS = 16
PROJ_TILE = 512
MERGE_TILE = 512
A_Q_TILE = 256
C_Q_TILE = 128
MOE_ROWS = 512
MOE_TOKEN_TILE = 256
ZERO_ROWS = 8


def _cparams(n_axes, vmem=None):
    return pltpu.CompilerParams(dimension_semantics=("arbitrary",) * n_axes,
                                vmem_limit_bytes=vmem)


def _const_spec(shape):
    nd = len(shape)
    return pl.BlockSpec(shape, lambda *_: (0,) * nd, pipeline_mode=pl.Buffered(1))


def _ada_kernel(c_ref, w_ref, b_ref, o_ref):
    c = c_ref[...]
    s = c * jax.nn.sigmoid(c)
    o_ref[0] = jnp.dot(s, w_ref[0], preferred_element_type=F32,
                       precision=lax.Precision.HIGHEST) + b_ref[0]


def _ada_modulation(c_rows, w_ada, b_ada):
    depth, d, _ = w_ada.shape
    return pl.pallas_call(
        _ada_kernel,
        grid=(depth, 6),
        in_specs=[pl.BlockSpec((ADA_ROWS, d), lambda l, j: (0, 0)),
                  pl.BlockSpec((1, d, d), lambda l, j: (l, 0, j)),
                  pl.BlockSpec((1, 1, d), lambda l, j: (l, 0, j))],
        out_specs=pl.BlockSpec((1, ADA_ROWS, d), lambda l, j: (l, 0, j)),
        out_shape=jax.ShapeDtypeStruct((depth, ADA_ROWS, 6 * d), F32),
        compiler_params=_cparams(2),
    )(c_rows, w_ada, b_ada.reshape(depth, 1, 6 * d))


def _row_rms(x, gain):
    return x * lax.rsqrt(jnp.mean(x * x, axis=-1, keepdims=True) + RMS_EPS) * gain


def _head_rms(t, gain, bd):
    ss = jnp.dot((t * t).astype(BF16), bd, preferred_element_type=F32)
    return t * lax.rsqrt(ss * (1.0 / HEAD_DIM) + RMS_EPS) * gain


def _proj_kernel(x_ref, shift_ref, scale_ref, gmix_ref,
                 w_aq, w_ak, w_av, w_bu, w_bv, w_cq, w_ck, w_cv, w_g,
                 bd_ref, aqn, akn, cqn, ckn, lng, lnb,
                 aq_o, ak_o, av_o, bu_o, bv_o, cq_o, ck_o, cv_o, g_o):
    x = x_ref[0]
    h = _row_rms(x, gmix_ref[...]) * (1.0 + scale_ref[0]) + shift_ref[0]
    hb = h.astype(BF16)

    def mm(w_ref):
        return jnp.dot(hb, w_ref[...], preferred_element_type=F32)

    bd = bd_ref[...]
    q_scale = HEAD_DIM ** -0.5
    aq_o[0] = (_head_rms(mm(w_aq), aqn[...], bd) * q_scale).astype(BF16)
    ak_o[0] = _head_rms(mm(w_ak), akn[...], bd)
    av_o[0] = mm(w_av)
    bu_o[0] = jax.nn.gelu(mm(w_bu)).astype(BF16)
    bv = jax.nn.gelu(mm(w_bv))
    mu = jnp.mean(bv, axis=-1, keepdims=True)
    var = jnp.mean(jnp.square(bv - mu), axis=-1, keepdims=True)
    bv_o[0] = (bv - mu) * lax.rsqrt(var + LN_EPS) * lng[...] + lnb[...]
    cq_o[0] = (_head_rms(mm(w_cq), cqn[...], bd) * q_scale).astype(BF16)
    kvw = ck_o.shape[-1]
    ck_o[0] = _head_rms(mm(w_ck), ckn[...], bd[:kvw, :kvw])
    cv_o[0] = mm(w_cv)
    d = x.shape[-1]
    for r in range(N_BRANCH):
        g_o[0, :, r * d:(r + 1) * d] = jax.nn.sigmoid(
            jnp.dot(hb, w_g[:, r * d:(r + 1) * d], preferred_element_type=F32)).astype(BF16)


def _project(x, shift, scale, lw, tm):
    b, s, d = x.shape
    r = shift.shape[1]
    widths = [w.shape[1] for w in lw["w_split"]]
    aw, kvw = widths[0], widths[6]
    tok = lambda w: pl.BlockSpec((1, tm, w), lambda bi, i: (bi, i, 0))
    mod = pl.BlockSpec((1, r, d), lambda bi, i: (bi, 0, 0))
    in_specs = ([tok(d), mod, mod, _const_spec((1, d))]
                + [_const_spec(w.shape) for w in lw["w_split"]]
                + [_const_spec(lw["bd"].shape)]
                + [_const_spec((1, aw)), _const_spec((1, aw)), _const_spec((1, aw)),
                   _const_spec((1, kvw)), _const_spec((1, aw)), _const_spec((1, aw))])
    out_dt = [BF16, F32, F32, BF16, F32, BF16, F32, F32, BF16]
    return pl.pallas_call(
        _proj_kernel,
        grid=(b, s // tm),
        in_specs=in_specs,
        out_specs=[tok(w) for w in widths],
        out_shape=[jax.ShapeDtypeStruct((b, s, w), dt) for w, dt in zip(widths, out_dt)],
        compiler_params=_cparams(2, VMEM_LIMIT_BYTES),
    )(x, shift, scale, lw["g_mix"], *lw["w_split"], lw["bd"],
      lw["aqn"], lw["akn"], lw["cqn"], lw["ckn"], lw["lng"], lw["lnb"])


def _attn_kernel(*refs, n_q_heads, n_kv_heads, has_sink, carry):
    refs = list(refs)
    q_ref, k_ref, v_ref = refs[:3]
    pos = 3
    if not carry:
        pk_ref, pv_ref = refs[pos:pos + 2]
        pos += 2
    bias_ref = refs[pos]
    pos += 1
    if has_sink:
        sink_ref = refs[pos]
        pos += 1
    o_ref = refs[pos]
    pos += 1
    if carry:
        pk_scr, pv_scr = refs[pos:pos + 2]

    j = pl.program_id(1)
    kc = k_ref[0].astype(BF16)
    vc = v_ref[0].astype(BF16)
    tq = kc.shape[0]
    if carry:
        @pl.when(j == 0)
        def _():
            pk_scr[...] = jnp.zeros_like(pk_scr)
            pv_scr[...] = jnp.zeros_like(pv_scr)
        kp, vp = pk_scr[...], pv_scr[...]
    else:
        kp, vp = pk_ref[0].astype(BF16), pv_ref[0].astype(BF16)
    n_prev = kp.shape[0]
    kw = jnp.concatenate([kp, kc], axis=0)
    vw = jnp.concatenate([vp, vc], axis=0)
    width = n_prev + tq
    if carry:
        before_start = lax.broadcasted_iota(jnp.int32, (tq, width), 1) < (n_prev - j * tq)
    q = q_ref[0]
    group = n_q_heads // n_kv_heads
    outs = []
    for h in range(n_q_heads):
        kh = h // group
        qh = q[:, h * HEAD_DIM:(h + 1) * HEAD_DIM]
        s = lax.dot_general(qh, kw[:, kh * HEAD_DIM:(kh + 1) * HEAD_DIM],
                            (((1,), (1,)), ((), ())), preferred_element_type=F32) + bias_ref[h]
        if carry:
            s = jnp.where(before_start, NEG_INF, s)
        m = jnp.max(s, axis=-1, keepdims=True)
        if has_sink:
            sink = sink_ref[h]
            m = jnp.maximum(m, sink)
        e = jnp.exp(s - m)
        denom = jnp.sum(e, axis=-1, keepdims=True)
        if has_sink:
            denom = denom + jnp.exp(sink - m)
        o = jnp.dot(e.astype(BF16), vw[:, kh * HEAD_DIM:(kh + 1) * HEAD_DIM],
                    preferred_element_type=F32)
        outs.append(o / denom)
    o_ref[0] = jnp.concatenate(outs, axis=-1).astype(BF16)
    if carry:
        pk_scr[...] = kw[tq:, :]
        pv_scr[...] = vw[tq:, :]


def _band_attention(q, k, v, bias, sinks, n_prev_rows, tq, prev=None):
    b, s, qw = q.shape
    kvw = k.shape[-1]
    hq, kvh = qw // HEAD_DIM, kvw // HEAD_DIM
    carry = prev is None
    width = n_prev_rows + tq
    tok = lambda w: pl.BlockSpec((1, tq, w), lambda bi, i: (bi, i, 0))
    in_specs = [tok(qw), tok(kvw), tok(kvw)]
    args = [q, k, v]
    if not carry:
        pspec = pl.BlockSpec((1, n_prev_rows, kvw), lambda bi, i: (bi, 0, 0))
        in_specs += [pspec, pspec]
        args += list(prev)
    in_specs.append(_const_spec((hq, tq, width)))
    args.append(bias)
    if sinks is not None:
        in_specs.append(pl.BlockSpec(memory_space=pltpu.SMEM))
        args.append(sinks)
    scratch = [pltpu.VMEM((n_prev_rows, kvw), BF16)] * 2 if carry else []
    return pl.pallas_call(
        functools.partial(_attn_kernel, n_q_heads=hq, n_kv_heads=kvh,
                          has_sink=sinks is not None, carry=carry),
        grid=(b, s // tq),
        in_specs=in_specs,
        out_specs=tok(qw),
        out_shape=jax.ShapeDtypeStruct((b, s, qw), BF16),
        scratch_shapes=scratch,
        compiler_params=_cparams(2, VMEM_LIMIT_BYTES),
    )(*args)


def _band_bias(base, tq, n_prev_chunks):
    n_qc = tq // CHUNK
    blocks = [jnp.pad(base, ((0, 0), (0, 0), (jq * CHUNK, (n_qc - 1 - jq) * CHUNK)),
                      constant_values=NEG_INF) for jq in range(n_qc)]
    return jnp.concatenate(blocks, axis=1)


def _toeplitz(vec, n_q, n_k):
    return jnp.stack([vec[:, n_q - 1 - i:n_q - 1 - i + n_k] for i in range(n_q)], axis=1)


def _a_bias_table(a_rel, q0, n_q, n_k):
    t = np.arange(n_q + n_k - 1)
    idx = np.clip(q0 + n_q - 1 - t, -A_REL_CLIP, A_REL_CLIP) + A_REL_CLIP
    return _toeplitz(a_rel[:, idx].astype(F32), n_q, n_k)


def _t5_bucket(rel):
    n = T5_BUCKETS // 2
    max_exact = n // 2
    ret = jnp.where(rel > 0, n, 0)
    a = jnp.abs(rel)
    large = max_exact + (jnp.log(jnp.maximum(a, 1).astype(F32) / max_exact)
                         / math.log(T5_MAX_DIST / max_exact) * (n - max_exact)).astype(jnp.int32)
    large = jnp.minimum(large, n - 1)
    return ret + jnp.where(a < max_exact, a, large)


def _c_bias_table(t5_table, q0, n_q, n_k):
    rel = jnp.arange(n_q + n_k - 1, dtype=jnp.int32) - (n_q - 1) - q0
    vec = jnp.transpose(t5_table[_t5_bucket(rel)]).astype(F32)
    return _toeplitz(vec, n_q, n_k)


def _merge_kernel(oa_ref, oc_ref, bu_ref, bv_ref, g_ref, x_ref, gate1_ref, shift2_ref, scale2_ref,
                  gffn_ref, wbr_ref, wout_ref, ws_ref, bsb_ref, wrt_ref, br_ref, tri_ref, cnt_ref,
                  xnew_o, h2_o, ids_o, gates_o, rank_o, cnt_o, base_scr):
    first = jnp.logical_and(pl.program_id(0) == 0, pl.program_id(1) == 0)

    @pl.when(first)
    def _():
        base_scr[...] = cnt_ref[...]

    tm, d = x_ref.shape[1], x_ref.shape[2]
    bw = bv_ref.shape[2]
    gch = bw // B_GROUPS
    bvb = bv_ref[0].astype(BF16)
    rows = []
    for r in range(tm // B_CHUNK):
        cols = [jnp.dot(ws_ref[gi], bvb[r * B_CHUNK:(r + 1) * B_CHUNK, gi * gch:(gi + 1) * gch],
                        preferred_element_type=F32) for gi in range(B_GROUPS)]
        rows.append(jnp.concatenate(cols, axis=-1) + bsb_ref[...])
    mixed = jnp.concatenate(rows, axis=0) if len(rows) > 1 else rows[0]
    ob = (bu_ref[0].astype(F32) * mixed).astype(BF16)

    branches = (oa_ref[0], ob, oc_ref[0])
    y = None
    for r in range(N_BRANCH):
        br = jnp.dot(branches[r], wbr_ref[r], preferred_element_type=F32)
        term = g_ref[0, :, r * d:(r + 1) * d].astype(F32) * br
        y = term if y is None else y + term
    out = jnp.dot(y.astype(BF16), wout_ref[...], preferred_element_type=F32)
    xn = x_ref[0] + gate1_ref[0] * out
    xnew_o[0] = xn
    h2 = _row_rms(xn, gffn_ref[...]) * (1.0 + scale2_ref[0]) + shift2_ref[0]
    h2_o[0] = h2

    nt = (((1,), (1,)), ((), ()))
    h_hi = h2.astype(BF16)
    h_lo = (h2 - h_hi.astype(F32)).astype(BF16)
    logits = (lax.dot_general(wrt_ref[0], h_hi, nt, preferred_element_type=F32)
              + lax.dot_general(wrt_ref[0], h_lo, nt, preferred_element_type=F32)
              + lax.dot_general(wrt_ref[1], h_hi, nt, preferred_element_type=F32)) + br_ref[...]
    n_exp = logits.shape[0]
    eidx = lax.broadcasted_iota(jnp.int32, (n_exp, tm), 0).astype(F32)
    vals, ids = [], []
    rest = logits
    for _ in range(TOP_K):
        m = jnp.max(rest, axis=0, keepdims=True)
        sel = jnp.min(jnp.where(rest == m, eidx, float(n_exp)), axis=0, keepdims=True)
        vals.append(m)
        ids.append(sel)
        rest = jnp.where(eidx == sel, F32_LOWEST, rest)
    exps = [jnp.exp(v - vals[0]) for v in vals]
    tot = exps[0] + exps[1] + exps[2] + exps[3]
    gates_o[0] = jnp.concatenate([e / tot for e in exps], axis=0)
    ids_o[0] = jnp.concatenate(ids, axis=0).astype(jnp.int32)

    base = base_scr[...]
    ranks = []
    for k in range(TOP_K):
        hit = eidx == ids[k]
        onehot = jnp.where(hit, 1.0, 0.0)
        before = jnp.dot(onehot.astype(BF16), tri_ref[...], preferred_element_type=F32)
        ranks.append(jnp.sum(jnp.where(hit, base[:, 0:1] + before, 0.0), axis=0, keepdims=True))
        base = base + jnp.sum(onehot, axis=1, keepdims=True)
    rank_o[0] = jnp.concatenate(ranks, axis=0).astype(jnp.int32)
    base_scr[...] = base
    cnt_o[...] = base


def _merge(oa, oc, bu, bv, g, x, gate1, shift2, scale2, lw, sgu_w, sgu_b, counts, tm):
    b, s, d = x.shape
    r = gate1.shape[1]
    bw = bu.shape[-1]
    n_tiles = b * (s // tm)
    tok = lambda w: pl.BlockSpec((1, tm, w), lambda bi, i: (bi, i, 0))
    mod = pl.BlockSpec((1, r, d), lambda bi, i: (bi, 0, 0))
    per_tile = pl.BlockSpec((1, TOP_K, tm), lambda bi, i: (bi * (s // tm) + i, 0, 0))
    tri = jnp.asarray(np.triu(np.ones((tm, tm), np.float32), k=1), BF16)
    in_specs = [tok(bw), tok(bw), tok(bw), tok(bw), tok(N_BRANCH * d), tok(d), mod, mod, mod,
                _const_spec((1, d)), _const_spec(lw["w_br"].shape), _const_spec((d, d)),
                _const_spec(sgu_w.shape), _const_spec(sgu_b.shape),
                _const_spec((2, N_EXPERTS, d)), _const_spec((N_EXPERTS, 1)),
                _const_spec((tm, tm)), _const_spec((N_EXPERTS, 128))]
    out_shape = [jax.ShapeDtypeStruct((b, s, d), F32), jax.ShapeDtypeStruct((b, s, d), F32),
                 jax.ShapeDtypeStruct((n_tiles, TOP_K, tm), jnp.int32),
                 jax.ShapeDtypeStruct((n_tiles, TOP_K, tm), F32),
                 jax.ShapeDtypeStruct((n_tiles, TOP_K, tm), jnp.int32),
                 jax.ShapeDtypeStruct((N_EXPERTS, 128), F32)]
    out_specs = [tok(d), tok(d), per_tile, per_tile, per_tile, _const_spec((N_EXPERTS, 128))]
    return pl.pallas_call(
        _merge_kernel,
        grid=(b, s // tm),
        in_specs=in_specs,
        out_specs=out_specs,
        out_shape=out_shape,
        scratch_shapes=[pltpu.VMEM((N_EXPERTS, 128), F32)],
        compiler_params=_cparams(2, VMEM_LIMIT_BYTES),
    )(oa, oc, bu, bv, g, x, gate1, shift2, scale2, lw["g_ffn"], lw["w_br"], lw["w_out"],
      sgu_w, sgu_b, lw["w_rt"], lw["b_r"], tri, counts)


def _row_copy(src_ref, src_row, dst_ref, dst_row, sem):
    return pltpu.make_async_copy(src_ref.at[pl.ds(src_row, 1)], dst_ref.at[pl.ds(dst_row, 1)], sem)


def _scatter_rows(pos_ref, base, h_ref, xs_ref, sem):
    tm = h_ref.shape[0]
    for t in range(tm):
        for k in range(TOP_K):
            _row_copy(h_ref, t, xs_ref, pos_ref[base + t * TOP_K + k], sem).start(priority=k % 2)
    for _ in range(tm * TOP_K):
        _row_copy(h_ref, 0, xs_ref, 0, sem).wait()


def _dispatch_kernel(posp_ref, poss_ref, zstart_ref, zchunks_ref, hp_ref, hs_ref, xs_ref,
                     zero_scr, sem):
    i = pl.program_id(0)
    n_p = pl.num_programs(0) - 2
    zrows = zero_scr.shape[0]

    def zero_copy(row):
        return pltpu.make_async_copy(zero_scr, xs_ref.at[pl.ds(row, zrows)], sem)

    @pl.when(i == 0)
    def _():
        zero_scr[...] = jnp.zeros_like(zero_scr)

        def per_range(r, carry):
            start = zstart_ref[r]

            def issue(j, c):
                zero_copy(pl.multiple_of(start + j * zrows, zrows)).start()
                return c

            lax.fori_loop(0, zchunks_ref[r], issue, 0)
            return carry

        def drain_range(r, carry):
            def drain(j, c):
                zero_copy(0).wait()
                return c

            lax.fori_loop(0, zchunks_ref[r], drain, 0)
            return carry

        lax.fori_loop(0, zstart_ref.shape[0], per_range, 0)
        lax.fori_loop(0, zstart_ref.shape[0], drain_range, 0)

    @pl.when(jnp.logical_and(i >= 1, i <= n_p))
    def _():
        _scatter_rows(posp_ref, (i - 1) * (hp_ref.shape[0] * TOP_K), hp_ref, xs_ref, sem)

    @pl.when(i == n_p + 1)
    def _():
        _scatter_rows(poss_ref, 0, hs_ref, xs_ref, sem)


def _dispatch(pos_p, pos_s, zstart, zchunks, h_p, h_s, n_rows, tm):
    tp, d = h_p.shape
    ts = h_s.shape[0]
    n_p = tp // tm
    return pl.pallas_call(
        _dispatch_kernel,
        grid_spec=pltpu.PrefetchScalarGridSpec(
            num_scalar_prefetch=4,
            grid=(n_p + 2,),
            in_specs=[pl.BlockSpec((tm, d), lambda i, *_: (jnp.clip(i - 1, 0, n_p - 1), 0)),
                      pl.BlockSpec((ts, d), lambda i, *_: (0, 0))],
            out_specs=pl.BlockSpec(memory_space=pl.ANY),
            scratch_shapes=[pltpu.VMEM((ZERO_ROWS, d), F32), pltpu.SemaphoreType.DMA]),
        out_shape=jax.ShapeDtypeStruct((n_rows, d), F32),
        compiler_params=_cparams(1),
    )(pos_p, pos_s, zstart, zchunks, h_p, h_s)


def _expert_kernel(be_ref, nused_ref, xs_ref, wgu_ref, bgu_ref, wdn_ref, bdn_ref, ys_ref,
                   wgu_scr, wdn_scr):
    i = pl.program_id(0)
    prev_e = be_ref[jnp.maximum(i - 1, 0)]
    new_expert = jnp.logical_or(i == 0, be_ref[i] != prev_e)
    active = i < nused_ref[0]

    @pl.when(jnp.logical_and(active, new_expert))
    def _():
        wgu_scr[...] = wgu_ref[0].astype(BF16)
        wdn_scr[...] = wdn_ref[0].astype(BF16)

    @pl.when(active)
    def _():
        de = wdn_scr.shape[0]
        gu = jnp.dot(xs_ref[...].astype(BF16), wgu_scr[...],
                     preferred_element_type=F32) + bgu_ref[0]
        g = jnp.minimum(gu[:, :de], SWIGLU_LIMIT)
        u = jnp.clip(gu[:, de:], -SWIGLU_LIMIT, SWIGLU_LIMIT)
        act = (u + 1.0) * g * jax.nn.sigmoid(SWIGLU_ALPHA * g)
        ys_ref[...] = jnp.dot(act.astype(BF16), wdn_scr[...],
                              preferred_element_type=F32) + bdn_ref[0]

    @pl.when(jnp.logical_not(active))
    def _():
        ys_ref[...] = jnp.zeros_like(ys_ref)


def _experts(block_e, n_used, xs, w_gu, b_gu, w_dn, b_dn, rows):
    n_rows, d = xs.shape
    n_e, _, two_de = w_gu.shape
    de = two_de // 2
    n_blocks = n_rows // rows
    row_map = lambda i, be, nu: (jnp.minimum(i, nu[0] - 1), 0)
    e_map = lambda i, be, nu: (be[jnp.minimum(i, nu[0] - 1)], 0, 0)
    return pl.pallas_call(
        _expert_kernel,
        grid_spec=pltpu.PrefetchScalarGridSpec(
            num_scalar_prefetch=2,
            grid=(n_blocks,),
            in_specs=[pl.BlockSpec((rows, d), row_map),
                      pl.BlockSpec((1, d, two_de), e_map),
                      pl.BlockSpec((1, 1, two_de), e_map),
                      pl.BlockSpec((1, de, d), e_map),
                      pl.BlockSpec((1, 1, d), e_map)],
            out_specs=pl.BlockSpec((rows, d), lambda i, be, nu: (i, 0)),
            scratch_shapes=[pltpu.VMEM((d, two_de), BF16), pltpu.VMEM((de, d), BF16)]),
        out_shape=jax.ShapeDtypeStruct((n_rows, d), F32),
        compiler_params=_cparams(1, VMEM_LIMIT_BYTES),
    )(block_e, n_used, xs, w_gu, b_gu.reshape(n_e, 1, two_de), w_dn, b_dn.reshape(n_e, 1, d))


def _combine_kernel(pos_ref, ys_ref, gates_ref, x_ref, gate2_ref, o_ref, buf, sem):
    tm = x_ref.shape[1]
    n_i = pl.num_programs(1)
    g = pl.program_id(0) * n_i + pl.program_id(1)
    last = pl.num_programs(0) * n_i - 1
    slot = g % 2

    def gather(tile, to_slot):
        base = tile * (tm * TOP_K)
        for t in range(tm):
            for k in range(TOP_K):
                _row_copy(ys_ref, pos_ref[base + t * TOP_K + k], buf.at[to_slot, k], t,
                          sem.at[to_slot]).start(priority=k % 2)

    def drain(from_slot):
        for _ in range(tm * TOP_K):
            _row_copy(ys_ref, 0, buf.at[from_slot, 0], 0, sem.at[from_slot]).wait()

    @pl.when(g == 0)
    def _():
        gather(0, 0)

    gather(jnp.minimum(g + 1, last), 1 - slot)
    drain(slot)
    gates = gates_ref[0]
    ff = gates[:, 0:1] * buf[slot, 0]
    for k in range(1, TOP_K):
        ff = ff + gates[:, k:k + 1] * buf[slot, k]
    o_ref[0] = x_ref[0] + gate2_ref[0] * ff

    @pl.when(g == last)
    def _():
        drain(1 - slot)


def _combine(pos_flat, ys, gates_tk, x, gate2, tm):
    b, s, d = x.shape
    r = gate2.shape[1]
    return pl.pallas_call(
        _combine_kernel,
        grid_spec=pltpu.PrefetchScalarGridSpec(
            num_scalar_prefetch=1,
            grid=(b, s // tm),
            in_specs=[pl.BlockSpec(memory_space=pl.ANY),
                      pl.BlockSpec((1, tm, TOP_K), lambda bi, i, pos: (bi, i, 0)),
                      pl.BlockSpec((1, tm, d), lambda bi, i, pos: (bi, i, 0)),
                      pl.BlockSpec((1, r, d), lambda bi, i, pos: (bi, 0, 0))],
            out_specs=pl.BlockSpec((1, tm, d), lambda bi, i, pos: (bi, i, 0)),
            scratch_shapes=[pltpu.VMEM((2, TOP_K, tm, d), F32), pltpu.SemaphoreType.DMA((2,))]),
        out_shape=jax.ShapeDtypeStruct((b, s, d), F32),
        compiler_params=_cparams(2, VMEM_LIMIT_BYTES),
    )(pos_flat, ys, gates_tk, x, gate2)


def _route(ids, ranks, counts, rows, n_blocks):
    padded = (counts + rows - 1) // rows * rows
    pad_end = jnp.cumsum(padded)
    pad_start = pad_end - padded
    experts = jnp.arange(N_EXPERTS, dtype=jnp.int32)
    pos = jnp.sum(jnp.where(ids[..., None] == experts, pad_start, 0), axis=-1) + ranks
    first_row = jnp.arange(n_blocks, dtype=jnp.int32) * rows
    block_e = jnp.minimum(jnp.sum((pad_end[None, :] <= first_row[:, None]).astype(jnp.int32), axis=1),
                          N_EXPERTS - 1)
    n_used = jnp.maximum(pad_end[-1:] // rows, 1).astype(jnp.int32)
    group_end = (pad_start + counts) // ZERO_ROWS * ZERO_ROWS
    zstart = jnp.concatenate([group_end, pad_end[-1:]]).astype(jnp.int32)
    zstop = jnp.concatenate([pad_end, jnp.full((1,), n_blocks * rows, pad_end.dtype)])
    zchunks = ((zstop - zstart) // ZERO_ROWS).astype(jnp.int32)
    return pos.astype(jnp.int32), block_e, n_used, zstart, zchunks


def _tiles_to_tokens(a):
    n_tiles, k, tm = a.shape
    return jnp.transpose(a, (1, 0, 2)).reshape(k, n_tiles * tm)


def _split_hi_lo(w):
    hi = w.astype(BF16)
    return jnp.stack([hi, (w - hi.astype(F32)).astype(BF16)])


def _layer_weights(l, w_in, g_norm_mix, g_norm_ffn, a_q_norm, a_k_norm, b_ln_g, b_ln_b,
                   c_q_norm, c_k_norm, w_branch, w_out, w_router, b_router, widths):
    d = w_in.shape[1]
    aw, kvw = widths[0], widths[6]
    splits = np.cumsum(widths)[:-1].tolist()
    w_split = [w.astype(BF16) for w in jnp.split(w_in[l], splits, axis=-1)]
    head = np.arange(aw) // HEAD_DIM
    bd = jnp.asarray(head[:, None] == head[None, :], BF16)
    tile = lambda v, w: jnp.tile(v, w // HEAD_DIM).reshape(1, w)
    return dict(
        w_split=w_split, bd=bd,
        g_mix=g_norm_mix[l].reshape(1, d), g_ffn=g_norm_ffn[l].reshape(1, d),
        aqn=tile(a_q_norm[l], aw), akn=tile(a_k_norm[l], aw),
        cqn=tile(c_q_norm[l], aw), ckn=tile(c_k_norm[l], kvw),
        lng=b_ln_g[l].reshape(1, aw), lnb=b_ln_b[l].reshape(1, aw),
        w_br=w_branch[l].astype(BF16), w_out=w_out[l].astype(BF16),
        w_rt=_split_hi_lo(jnp.transpose(w_router[l])), b_r=b_router[l].reshape(N_EXPERTS, 1))


def kernel(x_prompt, x_sample, cache_a_k, cache_a_v, cache_c_k, cache_c_v, c_prompt, c_sample,
           w_ada, b_ada, g_norm_mix, g_norm_ffn, w_in, a_q_norm, a_k_norm, a_rel_bias,
           b_ln_g, b_ln_b, b_w_s, b_b_s, c_q_norm, c_k_norm, c_sinks, t5_bias,
           w_branch, w_out, w_router, b_router, w_gate_up, b_gate_up, w_down, b_down):
    bp, sp, d = x_prompt.shape
    bs, ss, _ = x_sample.shape
    depth = w_in.shape[0]
    ts = bs * ss
    tp = bp * sp
    a_heads, c_kv_heads = cache_a_k.shape[3], cache_c_k.shape[3]
    aw = a_heads * HEAD_DIM
    kvw = c_kv_heads * HEAD_DIM
    widths = [aw, aw, aw, aw, aw, aw, kvw, kvw, N_BRANCH * d]
    ra, rc = cache_a_k.shape[2], cache_c_k.shape[2]
    a_band, c_band = A_PREV_CHUNKS * CHUNK, C_PREV_CHUNKS * CHUNK
    proj_tile, merge_tile = min(PROJ_TILE, sp), min(MERGE_TILE, sp)
    a_tile, c_tile = min(A_Q_TILE, sp), min(C_Q_TILE, sp)
    moe_tile = min(MOE_TOKEN_TILE, tp)

    c_rows = jnp.concatenate([c_prompt, c_sample,
                              jnp.zeros((ADA_ROWS - bp - bs, d), F32)], axis=0)
    mod_all = _ada_modulation(c_rows, w_ada, b_ada).reshape(depth, ADA_ROWS, 6, d)

    t5_p = _c_bias_table(t5_bias, c_band, CHUNK, c_band + CHUNK)
    t5_s = _c_bias_table(t5_bias, rc, ss, rc + ss)
    bias_c_p = _band_bias(t5_p, c_tile, C_PREV_CHUNKS)

    n_assign = (tp + ts) * TOP_K
    n_blocks = -(-(n_assign + N_EXPERTS * (MOE_ROWS - 1)) // MOE_ROWS)
    n_rows = n_blocks * MOE_ROWS

    mask = (np.arange(B_CHUNK)[None, :] // CHUNK <= np.arange(B_CHUNK)[:, None] // CHUNK)
    mask = jnp.asarray(mask, F32)

    xp = x_prompt
    xs = x_sample.reshape(1, ts, d)
    st_p, st_s = [], []
    for l in range(depth):
        lw = _layer_weights(l, w_in, g_norm_mix, g_norm_ffn, a_q_norm, a_k_norm, b_ln_g, b_ln_b,
                            c_q_norm, c_k_norm, w_branch, w_out, w_router, b_router, widths)
        mod_p = mod_all[l, :bp]
        mod_s = jnp.repeat(mod_all[l, bp:bp + bs], ss, axis=0)
        mp = [mod_p[:, i].reshape(bp, 1, d) for i in range(6)]
        ms = [mod_s[:, i].reshape(1, ts, d) for i in range(6)]

        ws_m = b_w_s[l] * mask
        sgu_w_p = ws_m.astype(BF16)
        sgu_b_p = jnp.repeat(jnp.transpose(b_b_s[l]), aw // B_GROUPS, axis=1)
        eye = jnp.eye(B_CHUNK // ss, dtype=F32)
        sgu_w_s = jnp.stack([jnp.kron(eye, ws_m[g, :ss, :ss]) for g in range(B_GROUPS)]).astype(BF16)
        sgu_b_s = jnp.tile(sgu_b_p[:ss], (B_CHUNK // ss, 1))

        bias_a_p = _band_bias(_a_bias_table(a_rel_bias[l], a_band, CHUNK, a_band + CHUNK),
                              a_tile, A_PREV_CHUNKS)
        bias_a_s = _a_bias_table(a_rel_bias[l], ra, ss, ra + ss)
        sinks = c_sinks[l].astype(F32)

        aq, ak, av, bu, bv, cq, ck, cv, g = _project(xp, mp[0], mp[1], lw, proj_tile)
        oa = _band_attention(aq, ak, av, bias_a_p, None, a_band, a_tile)
        oc = _band_attention(cq, ck, cv, bias_c_p, sinks, c_band, c_tile)
        na, nc = min(a_band, sp), min(c_band, sp)
        st_p.append((ak[:, sp - na:].reshape(bp, na, a_heads, HEAD_DIM),
                     av[:, sp - na:].reshape(bp, na, a_heads, HEAD_DIM),
                     ck[:, sp - nc:].reshape(bp, nc, c_kv_heads, HEAD_DIM),
                     cv[:, sp - nc:].reshape(bp, nc, c_kv_heads, HEAD_DIM)))
        zero_counts = jnp.zeros((N_EXPERTS, 128), F32)
        xp_mid, h2_p, ids_p, gates_p, rank_p, counts_p = _merge(
            oa, oc, bu, bv, g, xp, mp[2], mp[3], mp[4], lw, sgu_w_p, sgu_b_p, zero_counts,
            merge_tile)

        aq, ak, av, bu, bv, cq, ck, cv, g = _project(xs, ms[0], ms[1], lw, ts)
        per_seq = lambda a: a.reshape(bs, ss, a.shape[-1])
        oa = _band_attention(per_seq(aq), per_seq(ak), per_seq(av), bias_a_s, None, ra, ss,
                             prev=(cache_a_k[l].reshape(bs, ra, aw), cache_a_v[l].reshape(bs, ra, aw)))
        oc = _band_attention(per_seq(cq), per_seq(ck), per_seq(cv), t5_s, sinks, rc, ss,
                             prev=(cache_c_k[l].reshape(bs, rc, kvw), cache_c_v[l].reshape(bs, rc, kvw)))
        st_s.append((ak.reshape(bs, ss, a_heads, HEAD_DIM), av.reshape(bs, ss, a_heads, HEAD_DIM),
                     ck.reshape(bs, ss, c_kv_heads, HEAD_DIM), cv.reshape(bs, ss, c_kv_heads, HEAD_DIM),
                     bv.reshape(bs, ss, aw)))
        xs_mid, h2_s, ids_s, gates_s, rank_s, counts = _merge(
            oa.reshape(1, ts, aw), oc.reshape(1, ts, aw), bu, bv, g, xs, ms[2], ms[3], ms[4], lw,
            sgu_w_s, sgu_b_s, counts_p, ts)

        ids = jnp.concatenate([_tiles_to_tokens(ids_p), _tiles_to_tokens(ids_s)], axis=1)
        ranks = jnp.concatenate([_tiles_to_tokens(rank_p), _tiles_to_tokens(rank_s)], axis=1)
        pos, block_e, n_used, zstart, zchunks = _route(
            ids, ranks, counts[:, 0].astype(jnp.int32), MOE_ROWS, n_blocks)
        pos_tk = jnp.transpose(pos)
        pos_p = pos_tk[:tp].reshape(-1)
        pos_s = pos_tk[tp:].reshape(-1)
        rows_in = _dispatch(pos_p, pos_s, zstart, zchunks, h2_p.reshape(tp, d), h2_s.reshape(ts, d),
                            n_rows, moe_tile)
        n_e = w_gate_up.shape[1]
        rows_out = _experts(block_e + l * n_e, n_used, rows_in,
                            w_gate_up.reshape((depth * n_e,) + w_gate_up.shape[2:]),
                            b_gate_up.reshape(depth * n_e, -1),
                            w_down.reshape((depth * n_e,) + w_down.shape[2:]),
                            b_down.reshape(depth * n_e, -1), MOE_ROWS)
        gates_p_tk = jnp.transpose(_tiles_to_tokens(gates_p)).reshape(bp, sp, TOP_K)
        gates_s_tk = jnp.transpose(_tiles_to_tokens(gates_s)).reshape(1, ts, TOP_K)
        xp = _combine(pos_p, rows_out, gates_p_tk, xp_mid, mp[5], moe_tile)
        xs = _combine(pos_s, rows_out, gates_s_tk, xs_mid, ms[5], ts)

    stack = lambda rows, i: jnp.stack([r[i] for r in rows], axis=0)
    return (xp, xs.reshape(bs, ss, d),
            stack(st_p, 0), stack(st_p, 1), stack(st_p, 2), stack(st_p, 3),
            stack(st_s, 0), stack(st_s, 1), stack(st_s, 2), stack(st_s, 3), stack(st_s, 4))
```
